```python
import math
import jax, jax.numpy as jnp
from jax import lax
import numpy as np

D_MODEL = 1024
BATCH = 4
SEQ = 4096
DEPTH = 2

D_MIX = D_MODEL
POOL_WINDOWS = (2, 4, 8, 16)
N_POOL_GROUPS = len(POOL_WINDOWS)
POOL_WIDTH = D_MIX // 4
POOL_GROUP = POOL_WIDTH // N_POOL_GROUPS
DIFF_WIDTH = D_MIX // 2
DIFF_HEAD_DIM = 64
DIFF_HEADS = DIFF_WIDTH // (2 * DIFF_HEAD_DIM)
SB_WIDTH = D_MIX - POOL_WIDTH - DIFF_WIDTH
SB_HEAD_DIM = 64
SB_HEADS = SB_WIDTH // SB_HEAD_DIM
D_IN = POOL_WIDTH + 3 * DIFF_WIDTH + 3 * SB_WIDTH
D_FF = -(-(8 * D_MODEL) // (3 * 256)) * 256
Q_BLOCK = 128
EPS = 1e-6

kernel_name = "hybrid_pool_diffattn_stickbreaking_block"


def rmsnorm(x, g):
    x32 = x.astype(jnp.float32)
    y = x32 * lax.rsqrt(jnp.mean(x32 * x32, axis=-1, keepdims=True) + EPS)
    return y.astype(x.dtype) * g


def pool_mixer(u, pool_w, pool_scale):
    B_, S, _ = u.shape
    ug = u.reshape(B_, S, N_POOL_GROUPS, POOL_GROUP)
    cs = jnp.cumsum(ug.astype(jnp.float32), axis=1)
    cs = jnp.concatenate([jnp.zeros_like(cs[:, :1]), cs], axis=1)
    t = jnp.arange(S)[:, None]
    w = jnp.array(POOL_WINDOWS, dtype=jnp.int32)[None, :]
    lo = jnp.maximum(t + 1 - w, 0)
    cnt = jnp.minimum(t + 1, w).astype(jnp.float32)[..., None]
    gidx = jnp.arange(N_POOL_GROUPS)[None, :]
    win_sum = cs[:, 1:] - cs[:, lo, gidx]
    pooled = (win_sum / cnt - ug.astype(jnp.float32)).astype(u.dtype)
    mixed = jnp.einsum('bsgc,gcd->bsgd', pooled, pool_w)
    return mixed.reshape(B_, S, POOL_WIDTH) * pool_scale


def diff_attention(q, k, v, lam, lam_init, g):
    B_, S = q.shape[0], q.shape[1]
    scale = DIFF_HEAD_DIM ** -0.5
    outs = []
    for start in range(0, S, Q_BLOCK):
        end = start + Q_BLOCK
        qb, kb, vb = q[:, start:end], k[:, :end], v[:, :end]
        s = jnp.einsum('bqhrd,bkhrd->bhrqk', qb, kb).astype(jnp.float32) * scale
        mask = jnp.arange(end)[None, :] <= (start + jnp.arange(Q_BLOCK))[:, None]
        p = jax.nn.softmax(jnp.where(mask, s, -jnp.inf), axis=-1)
        attn = p[:, :, 0] - lam * p[:, :, 1]
        outs.append(jnp.einsum('bhqk,bkhe->bqhe', attn.astype(vb.dtype), vb))
    o = jnp.concatenate(outs, axis=1)
    o = rmsnorm(o, g) * (1.0 - lam_init)
    return o.reshape(B_, S, DIFF_WIDTH)


def stick_breaking_attention(q, k, v, g):
    B_, S = q.shape[0], q.shape[1]
    scale = SB_HEAD_DIM ** -0.5
    outs = []
    for start in range(0, S, Q_BLOCK):
        end = start + Q_BLOCK
        qb, kb, vb = q[:, start:end], k[:, :end], v[:, :end]
        z = jnp.einsum('bqhd,bkhd->bhqk', qb, kb).astype(jnp.float32) * scale
        mask = jnp.arange(end)[None, :] < (start + jnp.arange(Q_BLOCK))[:, None]
        log_1mb = jnp.where(mask, jax.nn.log_sigmoid(-z), 0.0)
        suffix = lax.cumsum(log_1mb, axis=3, reverse=True) - log_1mb
        a = jnp.where(mask, jnp.exp(jax.nn.log_sigmoid(z) + suffix), 0.0)
        outs.append(jnp.einsum('bhqk,bkhd->bqhd', a.astype(vb.dtype), vb))
    o = jnp.concatenate(outs, axis=1)
    return rmsnorm(o, g).reshape(B_, S, SB_WIDTH)


def setup_inputs(seed: int = 0) -> dict:
    key = jax.random.key(seed)
    ks = jax.random.split(key, 18)
    f32 = jnp.float32
    nrm = lambda k, shape, s: (jax.random.normal(k, shape, f32) * s).astype(f32)
    return {
        "x": nrm(ks[0], (BATCH, SEQ, D_MODEL), 1.0),
        "norm1_g": 1.0 + nrm(ks[1], (DEPTH, D_MODEL), 0.02),
        "w_in": nrm(ks[2], (DEPTH, D_MODEL, D_IN), D_MODEL ** -0.5),
        "pool_w": nrm(ks[3], (DEPTH, N_POOL_GROUPS, POOL_GROUP, POOL_GROUP), POOL_GROUP ** -0.5),
        "pool_scale": 1.0 + nrm(ks[4], (DEPTH, POOL_WIDTH), 0.02),
        "lam_q1": nrm(ks[5], (DEPTH, DIFF_HEAD_DIM), 0.1),
        "lam_k1": nrm(ks[6], (DEPTH, DIFF_HEAD_DIM), 0.1),
        "lam_q2": nrm(ks[7], (DEPTH, DIFF_HEAD_DIM), 0.1),
        "lam_k2": nrm(ks[8], (DEPTH, DIFF_HEAD_DIM), 0.1),
        "diff_norm_g": 1.0 + nrm(ks[9], (DEPTH, 2 * DIFF_HEAD_DIM), 0.02),
        "sb_norm_g": 1.0 + nrm(ks[10], (DEPTH, SB_HEAD_DIM), 0.02),
        "w_out": nrm(ks[11], (DEPTH, D_MIX, D_MODEL), D_MIX ** -0.5),
        "norm2_g": 1.0 + nrm(ks[12], (DEPTH, D_MODEL), 0.02),
        "w_gate": nrm(ks[13], (DEPTH, D_MODEL, D_FF), D_MODEL ** -0.5),
        "w_up": nrm(ks[14], (DEPTH, D_MODEL, D_FF), D_MODEL ** -0.5),
        "w_down": nrm(ks[15], (DEPTH, D_FF, D_MODEL), D_FF ** -0.5),
        "final_norm_g": 1.0 + nrm(ks[16], (D_MODEL,), 0.02),
    }


def reference(x, norm1_g, w_in, pool_w, pool_scale, lam_q1, lam_k1, lam_q2, lam_k2,
              diff_norm_g, sb_norm_g, w_out, norm2_g, w_gate, w_up, w_down, final_norm_g):
    B_, S, _ = x.shape
    c0 = POOL_WIDTH
    c1 = c0 + 3 * DIFF_WIDTH
    for l in range(DEPTH):
        h = rmsnorm(x, norm1_g[l])
        p = jnp.einsum('bsd,de->bse', h, w_in[l])
        u = p[..., :c0]
        dq, dk, dv = jnp.split(p[..., c0:c1], 3, axis=-1)
        sq, sk, sv = jnp.split(p[..., c1:], 3, axis=-1)

        a_out = pool_mixer(u, pool_w[l], pool_scale[l])

        lam_init = 0.8 - 0.6 * math.exp(-0.3 * l)
        lam = (jnp.exp(jnp.sum(lam_q1[l] * lam_k1[l]).astype(jnp.float32))
               - jnp.exp(jnp.sum(lam_q2[l] * lam_k2[l]).astype(jnp.float32)) + lam_init)
        b_out = diff_attention(dq.reshape(B_, S, DIFF_HEADS, 2, DIFF_HEAD_DIM),
                               dk.reshape(B_, S, DIFF_HEADS, 2, DIFF_HEAD_DIM),
                               dv.reshape(B_, S, DIFF_HEADS, 2 * DIFF_HEAD_DIM),
                               lam, lam_init, diff_norm_g[l])

        c_out = stick_breaking_attention(sq.reshape(B_, S, SB_HEADS, SB_HEAD_DIM),
                                         sk.reshape(B_, S, SB_HEADS, SB_HEAD_DIM),
                                         sv.reshape(B_, S, SB_HEADS, SB_HEAD_DIM),
                                         sb_norm_g[l])

        mix = jnp.concatenate([a_out, b_out, c_out], axis=-1)
        x = x + jnp.einsum('bse,ed->bsd', mix, w_out[l])

        h2 = rmsnorm(x, norm2_g[l])
        gate = jnp.einsum('bsd,df->bsf', h2, w_gate[l])
        up = jnp.einsum('bsd,df->bsf', h2, w_up[l])
        x = x + jnp.einsum('bsf,fd->bsd', jax.nn.silu(gate) * up, w_down[l])
    return rmsnorm(x, final_norm_g)
```

```python
import functools
import math

import jax
import jax.numpy as jnp
from jax import lax
from jax.experimental import pallas as pl
from jax.experimental.pallas import tpu as pltpu

F32 = jnp.float32
BF16 = jnp.bfloat16

D_MODEL = 1024
POOL_WINDOWS = (2, 4, 8, 16)
POOL_WIDTH = 256
POOL_GROUP = 64
DIFF_WIDTH = 512
DIFF_HEAD_DIM = 64
DIFF_HEADS = 4
SB_WIDTH = 256
SB_HEAD_DIM = 64
SB_HEADS = 4
D_IN = POOL_WIDTH + 3 * DIFF_WIDTH + 3 * SB_WIDTH
D_FF = 2816
EPS = 1e-6

LANES = 128
N_SLABS = (D_IN - POOL_WIDTH) // LANES
SLAB_DQ, SLAB_DK, SLAB_DV = 0, 4, 8
SLAB_SQ, SLAB_SK, SLAB_SV = 12, 14, 16

TM_PROJ = 512
TM_FFN = 512
TP_POOL = 512
POOL_HALO = 16
TQ = 256
TK = 256
FF_CHUNK = 512
VMEM_LIMIT = 56 * 1024 * 1024

_NT = (((1,), (1,)), ((), ()))


def _resident(shape):
    return pl.BlockSpec(shape, lambda *_: (0,) * len(shape), pipeline_mode=pl.Buffered(1))


def _rms(x, g):
    ms = jnp.mean(x * x, axis=-1, keepdims=True)
    return x * lax.rsqrt(ms + EPS) * g


def _proj_kernel(x_ref, g_ref, w_ref, u_ref, p_ref):
    h = _rms(x_ref[...], g_ref[...]).astype(BF16)
    u_ref[...] = jnp.dot(h, w_ref[:, :POOL_WIDTH], preferred_element_type=F32)
    for c0 in range(0, N_SLABS, 4):
        lo = POOL_WIDTH + c0 * LANES
        n = min(4, N_SLABS - c0)
        acc = jnp.dot(h, w_ref[:, lo:lo + n * LANES], preferred_element_type=F32)
        for j in range(n):
            p_ref[c0 + j] = acc[:, j * LANES:(j + 1) * LANES].astype(BF16)


def _proj(x2d, g, w_bf16):
    t = x2d.shape[0]
    return pl.pallas_call(
        _proj_kernel,
        grid=(t // TM_PROJ,),
        in_specs=[
            pl.BlockSpec((TM_PROJ, D_MODEL), lambda i: (i, 0)),
            _resident((1, D_MODEL)),
            _resident((D_MODEL, D_IN)),
        ],
        out_specs=[
            pl.BlockSpec((TM_PROJ, POOL_WIDTH), lambda i: (i, 0)),
            pl.BlockSpec((N_SLABS, TM_PROJ, LANES), lambda i: (0, i, 0)),
        ],
        out_shape=[
            jax.ShapeDtypeStruct((t, POOL_WIDTH), F32),
            jax.ShapeDtypeStruct((N_SLABS, t, LANES), BF16),
        ],
        compiler_params=pltpu.CompilerParams(
            dimension_semantics=("arbitrary",), vmem_limit_bytes=VMEM_LIMIT),
        name="proj",
    )(x2d, g.reshape(1, D_MODEL), w_bf16)


def _pool_kernel(u_ref, halo_ref, w_ref, scale_ref, o_ref):
    i = pl.program_id(1)
    halo = jnp.where(i > 0, halo_ref[...], 0.0)
    e0 = jnp.concatenate([halo, u_ref[...]], axis=0)
    s2 = e0 + pltpu.roll(e0, 1, 0)
    s4 = s2 + pltpu.roll(s2, 2, 0)
    s8 = s4 + pltpu.roll(s4, 4, 0)
    s16 = s8 + pltpu.roll(s8, 8, 0)
    shape = e0.shape
    lane = lax.broadcasted_iota(jnp.int32, shape, 1)
    row = lax.broadcasted_iota(jnp.int32, shape, 0)
    g0, g1, g2 = lane < POOL_GROUP, lane < 2 * POOL_GROUP, lane < 3 * POOL_GROUP
    win = jnp.where(g0, s2, jnp.where(g1, s4, jnp.where(g2, s8, s16)))
    width = jnp.where(g0, POOL_WINDOWS[0],
                      jnp.where(g1, POOL_WINDOWS[1],
                                jnp.where(g2, POOL_WINDOWS[2], POOL_WINDOWS[3])))
    pos = i * TP_POOL + row - POOL_HALO
    cnt = jnp.maximum(jnp.minimum(pos + 1, width), 1).astype(F32)
    pooled = (win / cnt - e0)[POOL_HALO:]
    mixed = jnp.dot(pooled.astype(BF16), w_ref[...], preferred_element_type=F32)
    o_ref[...] = (mixed * scale_ref[...]).astype(BF16)


def _pool(u, w_blockdiag, scale, batch, seq):
    t = u.shape[0]
    nblk = seq // TP_POOL
    halo_per_blk = TP_POOL // POOL_HALO
    return pl.pallas_call(
        _pool_kernel,
        grid=(batch, nblk),
        in_specs=[
            pl.BlockSpec((TP_POOL, POOL_WIDTH), lambda b, i: (b * nblk + i, 0)),
            pl.BlockSpec((POOL_HALO, POOL_WIDTH),
                         lambda b, i: (jnp.maximum((b * nblk + i) * halo_per_blk - 1, 0), 0)),
            _resident((POOL_WIDTH, POOL_WIDTH)),
            _resident((1, POOL_WIDTH)),
        ],
        out_specs=pl.BlockSpec((TP_POOL, POOL_WIDTH), lambda b, i: (b * nblk + i, 0)),
        out_shape=jax.ShapeDtypeStruct((t, POOL_WIDTH), BF16),
        compiler_params=pltpu.CompilerParams(
            dimension_semantics=("arbitrary", "arbitrary"), vmem_limit_bytes=VMEM_LIMIT),
        name="pool",
    )(u, u, w_blockdiag, scale.reshape(1, POOL_WIDTH))


def _twice(a):
    return jnp.concatenate([a] * (TK // LANES), axis=1)


def _diff_kernel(q_ref, k_ref, v_ref, lamp_ref, g_ref, o_ref,
                 m_ref, l_ref, acc_ref, *, lam_init):
    qi = pl.program_id(2)
    q = q_ref[0] * jnp.asarray(DIFF_HEAD_DIM ** -0.5, BF16)
    lane = lax.broadcasted_iota(jnp.int32, (TK, LANES), 1)
    first_map = lane < DIFF_HEAD_DIM

    def scores(start):
        k = k_ref[0, pl.ds(start, TK), :]
        zero = jnp.zeros_like(k)
        s1 = lax.dot_general(q, jnp.where(first_map, k, zero), _NT, preferred_element_type=F32)
        s2 = lax.dot_general(q, jnp.where(first_map, zero, k), _NT, preferred_element_type=F32)
        return s1, s2

    start = pl.multiple_of(qi * TQ, TQ)
    row = lax.broadcasted_iota(jnp.int32, (TQ, TK), 0)
    col = lax.broadcasted_iota(jnp.int32, (TQ, TK), 1)
    causal = col <= row
    v = v_ref[0, pl.ds(start, TK), :]
    for r, s in enumerate(scores(start)):
        s = jnp.where(causal, s, -jnp.inf)
        m = jnp.max(s, axis=1, keepdims=True)
        p = jnp.exp(s - m)
        m_ref[r] = jnp.broadcast_to(m, (TQ, LANES))
        l_ref[r] = jnp.broadcast_to(jnp.sum(p, axis=1, keepdims=True), (TQ, LANES))
        acc_ref[r] = jnp.dot(p.astype(BF16), v, preferred_element_type=F32)

    def body(kb, carry):
        start = pl.multiple_of(kb * TK, TK)
        v = v_ref[0, pl.ds(start, TK), :]
        for r, s in enumerate(scores(start)):
            m_prev = m_ref[r]
            m_new = jnp.maximum(m_prev, jnp.max(s, axis=1, keepdims=True))
            alpha = jnp.exp(m_prev - m_new)
            p = jnp.exp(s - _twice(m_new))
            l_ref[r] = alpha * l_ref[r] + jnp.sum(p, axis=1, keepdims=True)
            acc_ref[r] = alpha * acc_ref[r] + jnp.dot(p.astype(BF16), v,
                                                      preferred_element_type=F32)
            m_ref[r] = m_new
        return carry

    lax.fori_loop(0, qi, body, 0)

    lamp = lamp_ref[...]
    lam = (jnp.exp(jnp.sum(lamp[0:1] * lamp[1:2], axis=1, keepdims=True))
           - jnp.exp(jnp.sum(lamp[2:3] * lamp[3:4], axis=1, keepdims=True)) + lam_init)
    o = acc_ref[0] / l_ref[0] - lam * (acc_ref[1] / l_ref[1])
    o_ref[...] = (_rms(o, g_ref[...]) * (1.0 - lam_init)).astype(BF16)


def _diff_attention(p3, lamp, g, lam_init, batch, seq):
    t = p3.shape[1]
    nq = seq // TQ
    return pl.pallas_call(
        functools.partial(_diff_kernel, lam_init=lam_init),
        grid=(batch, DIFF_HEADS, nq),
        in_specs=[
            pl.BlockSpec((1, TQ, LANES), lambda b, h, i: (SLAB_DQ + h, b * nq + i, 0)),
            pl.BlockSpec((1, seq, LANES), lambda b, h, i: (SLAB_DK + h, b, 0)),
            pl.BlockSpec((1, seq, LANES), lambda b, h, i: (SLAB_DV + h, b, 0)),
            _resident((4, DIFF_HEAD_DIM)),
            _resident((1, 2 * DIFF_HEAD_DIM)),
        ],
        out_specs=pl.BlockSpec((TQ, LANES), lambda b, h, i: (b * nq + i, h)),
        out_shape=jax.ShapeDtypeStruct((t, DIFF_WIDTH), BF16),
        scratch_shapes=[
            pltpu.VMEM((2, TQ, LANES), F32),
            pltpu.VMEM((2, TQ, LANES), F32),
            pltpu.VMEM((2, TQ, LANES), F32),
        ],
        compiler_params=pltpu.CompilerParams(
            dimension_semantics=("arbitrary",) * 3, vmem_limit_bytes=VMEM_LIMIT),
        name="diff_attn",
    )(p3, p3, p3, lamp, g.reshape(1, 2 * DIFF_HEAD_DIM))


def _sb_kernel(q_ref, k_ref, v_ref, g_ref, o_ref, tri_ref, carry_ref, acc_ref):
    b, pr, qi = pl.program_id(0), pl.program_id(1), pl.program_id(2)

    @pl.when((b == 0) & (pr == 0) & (qi == 0))
    def _():
        j = lax.broadcasted_iota(jnp.int32, (2 * TK, TK), 0)
        s = lax.broadcasted_iota(jnp.int32, (2 * TK, TK), 1)
        j = jnp.where(j >= TK, j - TK, j)
        tri_ref[...] = jnp.where(j >= s, 1.0, 0.0).astype(BF16)

    q = q_ref[0] * jnp.asarray(SB_HEAD_DIM ** -0.5, BF16)
    lane = lax.broadcasted_iota(jnp.int32, (TK, LANES), 1)
    row = lax.broadcasted_iota(jnp.int32, (TQ, TK), 0)
    col = lax.broadcasted_iota(jnp.int32, (TQ, TK), 1)
    strict = col < row

    def tile(start, diagonal):
        k = k_ref[0, pl.ds(start, TK), :]
        v = v_ref[0, pl.ds(start, TK), :]
        for e in range(2):
            head = (lane < SB_HEAD_DIM) if e == 0 else (lane >= SB_HEAD_DIM)
            z = lax.dot_general(q, jnp.where(head, k, jnp.zeros_like(k)), _NT,
                                preferred_element_type=F32)
            sp = jnp.maximum(z, 0.0) + jnp.log(1.0 + jnp.exp(-jnp.abs(z)))
            if diagonal:
                sp = jnp.where(strict, sp, 0.0)
            hi = sp.astype(BF16)
            lo = (sp - hi.astype(F32)).astype(BF16)
            c = jnp.dot(jnp.concatenate([hi, lo], axis=1), tri_ref[...],
                        preferred_element_type=F32)
            if not diagonal:
                c = c + _twice(carry_ref[e])
            a = jnp.exp(z - c)
            if diagonal:
                a = jnp.where(strict, a, 0.0)
            pv = jnp.dot(a.astype(BF16), v, preferred_element_type=F32)
            acc_ref[e] = pv if diagonal else acc_ref[e] + pv
            carry_ref[e] = jnp.broadcast_to(c[:, 0:1], (TQ, LANES))

    tile(pl.multiple_of(qi * TQ, TQ), True)

    def body(j, carry):
        tile(pl.multiple_of((qi - 1 - j) * TK, TK), False)
        return carry

    lax.fori_loop(0, qi, body, 0)

    lane_o = lax.broadcasted_iota(jnp.int32, (TQ, LANES), 1)
    first = lane_o < SB_HEAD_DIM
    o = jnp.where(first, acc_ref[0], acc_ref[1])
    sq = o * o
    ms = jnp.where(first,
                   jnp.sum(jnp.where(first, sq, 0.0), axis=1, keepdims=True),
                   jnp.sum(jnp.where(first, 0.0, sq), axis=1, keepdims=True)) / SB_HEAD_DIM
    o_ref[...] = (o * lax.rsqrt(ms + EPS) * g_ref[...]).astype(BF16)


def _sb_attention(p3, g, batch, seq):
    t = p3.shape[1]
    nq = seq // TQ
    npairs = SB_WIDTH // LANES
    return pl.pallas_call(
        _sb_kernel,
        grid=(batch, npairs, nq),
        in_specs=[
            pl.BlockSpec((1, TQ, LANES), lambda b, h, i: (SLAB_SQ + h, b * nq + i, 0)),
            pl.BlockSpec((1, seq, LANES), lambda b, h, i: (SLAB_SK + h, b, 0)),
            pl.BlockSpec((1, seq, LANES), lambda b, h, i: (SLAB_SV + h, b, 0)),
            _resident((1, LANES)),
        ],
        out_specs=pl.BlockSpec((TQ, LANES), lambda b, h, i: (b * nq + i, h)),
        out_shape=jax.ShapeDtypeStruct((t, SB_WIDTH), BF16),
        scratch_shapes=[
            pltpu.VMEM((2 * TK, TK), BF16),
            pltpu.VMEM((2, TQ, LANES), F32),
            pltpu.VMEM((2, TQ, LANES), F32),
        ],
        compiler_params=pltpu.CompilerParams(
            dimension_semantics=("arbitrary",) * 3, vmem_limit_bytes=VMEM_LIMIT),
        name="sb_attn",
    )(p3, p3, p3, jnp.concatenate([g, g]).reshape(1, LANES))


def _ffn_kernel(x_ref, a_ref, b_ref, c_ref, wo_ref, g2_ref, wg_ref, wu_ref, wd_ref, gf_ref,
                o_ref, x1_ref, act_ref, *, last):
    mix = jnp.concatenate([a_ref[...], b_ref[...], c_ref[...]], axis=1)
    x1_ref[...] = x_ref[...] + jnp.dot(mix, wo_ref[...], preferred_element_type=F32)
    h2 = _rms(x1_ref[...], g2_ref[...]).astype(BF16)
    for f0 in range(0, D_FF, FF_CHUNK):
        f1 = min(f0 + FF_CHUNK, D_FF)
        gate = jnp.dot(h2, wg_ref[:, f0:f1], preferred_element_type=F32)
        up = jnp.dot(h2, wu_ref[:, f0:f1], preferred_element_type=F32)
        act_ref[:, f0:f1] = (gate * (1.0 / (1.0 + jnp.exp(-gate))) * up).astype(BF16)
    x2 = x1_ref[...] + jnp.dot(act_ref[...], wd_ref[...], preferred_element_type=F32)
    o_ref[...] = _rms(x2, gf_ref[...]) if last else x2


def _ffn(x2d, a, b, c, wo, g2, wg, wu, wd, gf, last):
    t = x2d.shape[0]
    rows = lambda w: pl.BlockSpec((TM_FFN, w), lambda i: (i, 0))
    return pl.pallas_call(
        functools.partial(_ffn_kernel, last=last),
        grid=(t // TM_FFN,),
        in_specs=[
            rows(D_MODEL), rows(POOL_WIDTH), rows(DIFF_WIDTH), rows(SB_WIDTH),
            _resident((D_MODEL, D_MODEL)), _resident((1, D_MODEL)),
            _resident((D_MODEL, D_FF)), _resident((D_MODEL, D_FF)), _resident((D_FF, D_MODEL)),
            _resident((1, D_MODEL)),
        ],
        out_specs=rows(D_MODEL),
        out_shape=jax.ShapeDtypeStruct((t, D_MODEL), F32),
        scratch_shapes=[
            pltpu.VMEM((TM_FFN, D_MODEL), F32),
            pltpu.VMEM((TM_FFN, D_FF), BF16),
        ],
        compiler_params=pltpu.CompilerParams(
            dimension_semantics=("arbitrary",), vmem_limit_bytes=VMEM_LIMIT),
        name="ffn_last" if last else "ffn",
    )(x2d, a, b, c, wo, g2.reshape(1, D_MODEL), wg, wu, wd, gf.reshape(1, D_MODEL))


def kernel(x, norm1_g, w_in, pool_w, pool_scale, lam_q1, lam_k1, lam_q2, lam_k2,
           diff_norm_g, sb_norm_g, w_out, norm2_g, w_gate, w_up, w_down, final_norm_g):
    batch, seq, d = x.shape
    depth = w_in.shape[0]
    assert d == D_MODEL and seq % TQ == 0 and seq % TP_POOL == 0
    assert (batch * seq) % TM_PROJ == 0 and (batch * seq) % TM_FFN == 0

    x2d = x.reshape(batch * seq, d)
    for l in range(depth):
        u, p3 = _proj(x2d, norm1_g[l], w_in[l].astype(BF16))
        w_bd = jax.scipy.linalg.block_diag(*[pool_w[l, gi] for gi in range(len(POOL_WINDOWS))])
        a = _pool(u, w_bd.astype(BF16), pool_scale[l], batch, seq)
        lam_init = 0.8 - 0.6 * math.exp(-0.3 * l)
        lamp = jnp.stack([lam_q1[l], lam_k1[l], lam_q2[l], lam_k2[l]])
        bo = _diff_attention(p3, lamp, diff_norm_g[l], lam_init, batch, seq)
        co = _sb_attention(p3, sb_norm_g[l], batch, seq)
        x2d = _ffn(x2d, a, bo, co, w_out[l].astype(BF16), norm2_g[l],
                   w_gate[l].astype(BF16), w_up[l].astype(BF16), w_down[l].astype(BF16),
                   final_norm_g, last=(l == depth - 1))
    return x2d.reshape(batch, seq, d)
```

```python
import functools
import math

import jax
import jax.numpy as jnp
from jax import lax
from jax.experimental import pallas as pl
from jax.experimental.pallas import tpu as pltpu

F32 = jnp.float32
BF16 = jnp.bfloat16

D_MODEL = 1024
POOL_WINDOWS = (2, 4, 8, 16)
POOL_WIDTH = 256
POOL_GROUP = 64
DIFF_WIDTH = 512
DIFF_HEAD_DIM = 64
DIFF_HEADS = 4
SB_WIDTH = 256
SB_HEAD_DIM = 64
SB_HEADS = 4
D_IN = POOL_WIDTH + 3 * DIFF_WIDTH + 3 * SB_WIDTH
D_FF = 2816
EPS = 1e-6

LANES = 128
N_SLABS = (D_IN - POOL_WIDTH) // LANES
SB_SLABS = SB_WIDTH // LANES
BLK_DQ, BLK_DK, BLK_DV = 0, 1, 2
BLK_SQ, BLK_SK, BLK_SV = 6, 7, 8

TM_PROJ = 512
TM_FFN = 512
TP_POOL = 512
POOL_HALO = 16
TQ = 256
TK = 256
FF_CHUNK = 512
VMEM_LIMIT = 56 * 1024 * 1024

_NT = (((1,), (1,)), ((), ()))


def _resident(shape):
    return pl.BlockSpec(shape, lambda *_: (0,) * len(shape), pipeline_mode=pl.Buffered(1))


def _rms(x, g):
    ms = jnp.mean(x * x, axis=-1, keepdims=True)
    return x * lax.rsqrt(ms + EPS) * g


def _proj_kernel(x_ref, g_ref, w_ref, u_ref, p_ref):
    h = _rms(x_ref[...], g_ref[...]).astype(BF16)
    u_ref[...] = jnp.dot(h, w_ref[:, :POOL_WIDTH], preferred_element_type=F32)
    for c0 in range(0, N_SLABS, 4):
        lo = POOL_WIDTH + c0 * LANES
        n = min(4, N_SLABS - c0)
        acc = jnp.dot(h, w_ref[:, lo:lo + n * LANES], preferred_element_type=F32)
        for j in range(n):
            p_ref[c0 + j] = acc[:, j * LANES:(j + 1) * LANES].astype(BF16)


def _proj(x2d, g, w_bf16):
    t = x2d.shape[0]
    return pl.pallas_call(
        _proj_kernel,
        grid=(t // TM_PROJ,),
        in_specs=[
            pl.BlockSpec((TM_PROJ, D_MODEL), lambda i: (i, 0)),
            _resident((1, D_MODEL)),
            _resident((D_MODEL, D_IN)),
        ],
        out_specs=[
            pl.BlockSpec((TM_PROJ, POOL_WIDTH), lambda i: (i, 0)),
            pl.BlockSpec((N_SLABS, TM_PROJ, LANES), lambda i: (0, i, 0)),
        ],
        out_shape=[
            jax.ShapeDtypeStruct((t, POOL_WIDTH), F32),
            jax.ShapeDtypeStruct((N_SLABS, t, LANES), BF16),
        ],
        compiler_params=pltpu.CompilerParams(
            dimension_semantics=("arbitrary",), vmem_limit_bytes=VMEM_LIMIT),
        name="proj",
    )(x2d, g.reshape(1, D_MODEL), w_bf16)


def _pool_kernel(u_ref, halo_ref, w_ref, scale_ref, o_ref):
    i = pl.program_id(1)
    halo = jnp.where(i > 0, halo_ref[...], 0.0)
    e0 = jnp.concatenate([halo, u_ref[...]], axis=0)
    s2 = e0 + pltpu.roll(e0, 1, 0)
    s4 = s2 + pltpu.roll(s2, 2, 0)
    s8 = s4 + pltpu.roll(s4, 4, 0)
    s16 = s8 + pltpu.roll(s8, 8, 0)
    shape = e0.shape
    lane = lax.broadcasted_iota(jnp.int32, shape, 1)
    row = lax.broadcasted_iota(jnp.int32, shape, 0)
    g0, g1, g2 = lane < POOL_GROUP, lane < 2 * POOL_GROUP, lane < 3 * POOL_GROUP
    win = jnp.where(g0, s2, jnp.where(g1, s4, jnp.where(g2, s8, s16)))
    width = jnp.where(g0, POOL_WINDOWS[0],
                      jnp.where(g1, POOL_WINDOWS[1],
                                jnp.where(g2, POOL_WINDOWS[2], POOL_WINDOWS[3])))
    pos = i * TP_POOL + row - POOL_HALO
    cnt = jnp.maximum(jnp.minimum(pos + 1, width), 1).astype(F32)
    pooled = (win / cnt - e0)[POOL_HALO:]
    mixed = jnp.dot(pooled.astype(BF16), w_ref[...], preferred_element_type=F32)
    o_ref[...] = (mixed * scale_ref[...]).astype(BF16)


def _pool(u, w_blockdiag, scale, batch, seq):
    t = u.shape[0]
    nblk = seq // TP_POOL
    halo_per_blk = TP_POOL // POOL_HALO
    return pl.pallas_call(
        _pool_kernel,
        grid=(batch, nblk),
        in_specs=[
            pl.BlockSpec((TP_POOL, POOL_WIDTH), lambda b, i: (b * nblk + i, 0)),
            pl.BlockSpec((POOL_HALO, POOL_WIDTH),
                         lambda b, i: (jnp.maximum((b * nblk + i) * halo_per_blk - 1, 0), 0)),
            _resident((POOL_WIDTH, POOL_WIDTH)),
            _resident((1, POOL_WIDTH)),
        ],
        out_specs=pl.BlockSpec((TP_POOL, POOL_WIDTH), lambda b, i: (b * nblk + i, 0)),
        out_shape=jax.ShapeDtypeStruct((t, POOL_WIDTH), BF16),
        compiler_params=pltpu.CompilerParams(
            dimension_semantics=("arbitrary", "arbitrary"), vmem_limit_bytes=VMEM_LIMIT),
        name="pool",
    )(u, u, w_blockdiag, scale.reshape(1, POOL_WIDTH))


def _twice(a):
    return jnp.concatenate([a] * (TK // LANES), axis=1)


def _lane_halves_sum(p):
    out = p[:, :LANES]
    for c in range(1, TK // LANES):
        out = out + p[:, c * LANES:(c + 1) * LANES]
    return out


def _diff_kernel(q_ref, k_ref, v_ref, lamp_ref, g_ref, o_ref,
                 m_ref, l_ref, acc_ref, *, lam_init):
    qi = pl.program_id(1)
    scale = jnp.asarray(DIFF_HEAD_DIM ** -0.5, BF16)
    qs = [q_ref[h] * scale for h in range(DIFF_HEADS)]
    lane = lax.broadcasted_iota(jnp.int32, (TK, LANES), 1)
    first_map = lane < DIFF_HEAD_DIM
    row = lax.broadcasted_iota(jnp.int32, (TQ, TK), 0)
    col = lax.broadcasted_iota(jnp.int32, (TQ, TK), 1)
    causal = col <= row

    def tile(start, diagonal):
        scores = []
        for h in range(DIFF_HEADS):
            k = k_ref[h, pl.ds(start, TK), :]
            zero = jnp.zeros_like(k)
            scores.append(lax.dot_general(qs[h], jnp.where(first_map, k, zero), _NT,
                                          preferred_element_type=F32))
            scores.append(lax.dot_general(qs[h], jnp.where(first_map, zero, k), _NT,
                                          preferred_element_type=F32))
        probs, alphas = [], []
        for i, s in enumerate(scores):
            if diagonal:
                s = jnp.where(causal, s, -jnp.inf)
                m_new = jnp.broadcast_to(jnp.max(s, axis=1, keepdims=True), (TQ, LANES))
            else:
                m_prev = m_ref[i]
                m_new = jnp.maximum(m_prev, jnp.max(s, axis=1, keepdims=True))
                alphas.append(jnp.exp(m_prev - m_new))
            p = jnp.exp(s - _twice(m_new))
            part = _lane_halves_sum(p)
            l_ref[i] = part if diagonal else alphas[i] * l_ref[i] + part
            m_ref[i] = m_new
            probs.append(p.astype(BF16))
        for i, p in enumerate(probs):
            pv = jnp.dot(p, v_ref[i // 2, pl.ds(start, TK), :], preferred_element_type=F32)
            acc_ref[i] = pv if diagonal else alphas[i] * acc_ref[i] + pv

    tile(pl.multiple_of(qi * TQ, TQ), True)

    def body(kb, carry):
        tile(pl.multiple_of(kb * TK, TK), False)
        return carry

    lax.fori_loop(0, qi, body, 0)

    lamp = lamp_ref[...]
    lam = (jnp.exp(jnp.sum(lamp[0:1] * lamp[1:2], axis=1, keepdims=True))
           - jnp.exp(jnp.sum(lamp[2:3] * lamp[3:4], axis=1, keepdims=True)) + lam_init)
    for h in range(DIFF_HEADS):
        l1 = jnp.sum(l_ref[2 * h], axis=1, keepdims=True)
        l2 = jnp.sum(l_ref[2 * h + 1], axis=1, keepdims=True)
        o = acc_ref[2 * h] / l1 - lam * (acc_ref[2 * h + 1] / l2)
        o_ref[:, h * LANES:(h + 1) * LANES] = (
            _rms(o, g_ref[...]) * (1.0 - lam_init)).astype(BF16)


def _diff_attention(p3, lamp, g, lam_init, batch, seq):
    t = p3.shape[1]
    nq = seq // TQ
    return pl.pallas_call(
        functools.partial(_diff_kernel, lam_init=lam_init),
        grid=(batch, nq),
        in_specs=[
            pl.BlockSpec((DIFF_HEADS, TQ, LANES), lambda b, i: (BLK_DQ, b * nq + i, 0)),
            pl.BlockSpec((DIFF_HEADS, seq, LANES), lambda b, i: (BLK_DK, b, 0)),
            pl.BlockSpec((DIFF_HEADS, seq, LANES), lambda b, i: (BLK_DV, b, 0)),
            _resident((4, DIFF_HEAD_DIM)),
            _resident((1, 2 * DIFF_HEAD_DIM)),
        ],
        out_specs=pl.BlockSpec((TQ, DIFF_WIDTH), lambda b, i: (b * nq + i, 0)),
        out_shape=jax.ShapeDtypeStruct((t, DIFF_WIDTH), BF16),
        scratch_shapes=[
            pltpu.VMEM((2 * DIFF_HEADS, TQ, LANES), F32),
            pltpu.VMEM((2 * DIFF_HEADS, TQ, LANES), F32),
            pltpu.VMEM((2 * DIFF_HEADS, TQ, LANES), F32),
        ],
        compiler_params=pltpu.CompilerParams(
            dimension_semantics=("arbitrary",) * 2, vmem_limit_bytes=VMEM_LIMIT),
        name="diff_attn",
    )(p3, p3, p3, lamp, g.reshape(1, 2 * DIFF_HEAD_DIM))


def _sb_kernel(q_ref, k_ref, v_ref, g_ref, o_ref, tri_ref, carry_ref, acc_ref):
    b, qi = pl.program_id(0), pl.program_id(1)

    @pl.when((b == 0) & (qi == 0))
    def _():
        j = lax.broadcasted_iota(jnp.int32, (2 * TK, TK), 0)
        s = lax.broadcasted_iota(jnp.int32, (2 * TK, TK), 1)
        j = jnp.where(j >= TK, j - TK, j)
        tri_ref[...] = jnp.where(j >= s, 1.0, 0.0).astype(BF16)

    scale = jnp.asarray(SB_HEAD_DIM ** -0.5, BF16)
    qs = [q_ref[sl] * scale for sl in range(SB_SLABS)]
    lane = lax.broadcasted_iota(jnp.int32, (TK, LANES), 1)
    lower = lane < SB_HEAD_DIM
    row = lax.broadcasted_iota(jnp.int32, (TQ, TK), 0)
    col = lax.broadcasted_iota(jnp.int32, (TQ, TK), 1)
    strict = col < row

    def tile(start, diagonal):
        zs = []
        for hd in range(SB_HEADS):
            sl, e = divmod(hd, 2)
            k = k_ref[sl, pl.ds(start, TK), :]
            zero = jnp.zeros_like(k)
            km = jnp.where(lower, k, zero) if e == 0 else jnp.where(lower, zero, k)
            zs.append(lax.dot_general(qs[sl], km, _NT, preferred_element_type=F32))
        splits = []
        for z in zs:
            sp = jnp.maximum(z, 0.0) + jnp.log(1.0 + jnp.exp(-jnp.abs(z)))
            if diagonal:
                sp = jnp.where(strict, sp, 0.0)
            hi = sp.astype(BF16)
            lo = (sp - hi.astype(F32)).astype(BF16)
            splits.append(jnp.concatenate([hi, lo], axis=1))
        sums = [jnp.dot(s, tri_ref[...], preferred_element_type=F32) for s in splits]
        weights = []
        for hd, (z, c) in enumerate(zip(zs, sums)):
            if not diagonal:
                c = c + _twice(carry_ref[hd])
            a = jnp.exp(z - c)
            if diagonal:
                a = jnp.where(strict, a, 0.0)
            carry_ref[hd] = jnp.broadcast_to(c[:, 0:1], (TQ, LANES))
            weights.append(a.astype(BF16))
        for hd, a in enumerate(weights):
            pv = jnp.dot(a, v_ref[hd // 2, pl.ds(start, TK), :], preferred_element_type=F32)
            acc_ref[hd] = pv if diagonal else acc_ref[hd] + pv

    tile(pl.multiple_of(qi * TQ, TQ), True)

    def body(j, carry):
        tile(pl.multiple_of((qi - 1 - j) * TK, TK), False)
        return carry

    lax.fori_loop(0, qi, body, 0)

    lane_o = lax.broadcasted_iota(jnp.int32, (TQ, LANES), 1)
    first = lane_o < SB_HEAD_DIM
    for sl in range(SB_SLABS):
        o = jnp.where(first, acc_ref[2 * sl], acc_ref[2 * sl + 1])
        sq = o * o
        ms = jnp.where(first,
                       jnp.sum(jnp.where(first, sq, 0.0), axis=1, keepdims=True),
                       jnp.sum(jnp.where(first, 0.0, sq), axis=1, keepdims=True)) / SB_HEAD_DIM
        o_ref[:, sl * LANES:(sl + 1) * LANES] = (
            o * lax.rsqrt(ms + EPS) * g_ref[...]).astype(BF16)


def _sb_attention(p3, g, batch, seq):
    t = p3.shape[1]
    nq = seq // TQ
    return pl.pallas_call(
        _sb_kernel,
        grid=(batch, nq),
        in_specs=[
            pl.BlockSpec((SB_SLABS, TQ, LANES), lambda b, i: (BLK_SQ, b * nq + i, 0)),
            pl.BlockSpec((SB_SLABS, seq, LANES), lambda b, i: (BLK_SK, b, 0)),
            pl.BlockSpec((SB_SLABS, seq, LANES), lambda b, i: (BLK_SV, b, 0)),
            _resident((1, LANES)),
        ],
        out_specs=pl.BlockSpec((TQ, SB_WIDTH), lambda b, i: (b * nq + i, 0)),
        out_shape=jax.ShapeDtypeStruct((t, SB_WIDTH), BF16),
        scratch_shapes=[
            pltpu.VMEM((2 * TK, TK), BF16),
            pltpu.VMEM((SB_HEADS, TQ, LANES), F32),
            pltpu.VMEM((SB_HEADS, TQ, LANES), F32),
        ],
        compiler_params=pltpu.CompilerParams(
            dimension_semantics=("arbitrary",) * 2, vmem_limit_bytes=VMEM_LIMIT),
        name="sb_attn",
    )(p3, p3, p3, jnp.concatenate([g, g]).reshape(1, LANES))


def _ffn_kernel(x_ref, a_ref, b_ref, c_ref, wo_ref, g2_ref, wg_ref, wu_ref, wd_ref, gf_ref,
                o_ref, x1_ref, act_ref, *, last):
    mix = jnp.concatenate([a_ref[...], b_ref[...], c_ref[...]], axis=1)
    x1_ref[...] = x_ref[...] + jnp.dot(mix, wo_ref[...], preferred_element_type=F32)
    h2 = _rms(x1_ref[...], g2_ref[...]).astype(BF16)
    for f0 in range(0, D_FF, FF_CHUNK):
        f1 = min(f0 + FF_CHUNK, D_FF)
        gate = jnp.dot(h2, wg_ref[:, f0:f1], preferred_element_type=F32)
        up = jnp.dot(h2, wu_ref[:, f0:f1], preferred_element_type=F32)
        act_ref[:, f0:f1] = (gate * (1.0 / (1.0 + jnp.exp(-gate))) * up).astype(BF16)
    x2 = x1_ref[...] + jnp.dot(act_ref[...], wd_ref[...], preferred_element_type=F32)
    o_ref[...] = _rms(x2, gf_ref[...]) if last else x2


def _ffn(x2d, a, b, c, wo, g2, wg, wu, wd, gf, last):
    t = x2d.shape[0]
    rows = lambda w: pl.BlockSpec((TM_FFN, w), lambda i: (i, 0))
    return pl.pallas_call(
        functools.partial(_ffn_kernel, last=last),
        grid=(t // TM_FFN,),
        in_specs=[
            rows(D_MODEL), rows(POOL_WIDTH), rows(DIFF_WIDTH), rows(SB_WIDTH),
            _resident((D_MODEL, D_MODEL)), _resident((1, D_MODEL)),
            _resident((D_MODEL, D_FF)), _resident((D_MODEL, D_FF)), _resident((D_FF, D_MODEL)),
            _resident((1, D_MODEL)),
        ],
        out_specs=rows(D_MODEL),
        out_shape=jax.ShapeDtypeStruct((t, D_MODEL), F32),
        scratch_shapes=[
            pltpu.VMEM((TM_FFN, D_MODEL), F32),
            pltpu.VMEM((TM_FFN, D_FF), BF16),
        ],
        compiler_params=pltpu.CompilerParams(
            dimension_semantics=("arbitrary",), vmem_limit_bytes=VMEM_LIMIT),
        name="ffn_last" if last else "ffn",
    )(x2d, a, b, c, wo, g2.reshape(1, D_MODEL), wg, wu, wd, gf.reshape(1, D_MODEL))


def kernel(x, norm1_g, w_in, pool_w, pool_scale, lam_q1, lam_k1, lam_q2, lam_k2,
           diff_norm_g, sb_norm_g, w_out, norm2_g, w_gate, w_up, w_down, final_norm_g):
    batch, seq, d = x.shape
    depth = w_in.shape[0]
    assert d == D_MODEL and seq % TQ == 0 and seq % TP_POOL == 0
    assert (batch * seq) % TM_PROJ == 0 and (batch * seq) % TM_FFN == 0

    x2d = x.reshape(batch * seq, d)
    for l in range(depth):
        u, p3 = _proj(x2d, norm1_g[l], w_in[l].astype(BF16))
        w_bd = jax.scipy.linalg.block_diag(*[pool_w[l, gi] for gi in range(len(POOL_WINDOWS))])
        a = _pool(u, w_bd.astype(BF16), pool_scale[l], batch, seq)
        lam_init = 0.8 - 0.6 * math.exp(-0.3 * l)
        lamp = jnp.stack([lam_q1[l], lam_k1[l], lam_q2[l], lam_k2[l]])
        bo = _diff_attention(p3, lamp, diff_norm_g[l], lam_init, batch, seq)
        co = _sb_attention(p3, sb_norm_g[l], batch, seq)
        x2d = _ffn(x2d, a, bo, co, w_out[l].astype(BF16), norm2_g[l],
                   w_gate[l].astype(BF16), w_up[l].astype(BF16), w_down[l].astype(BF16),
                   final_norm_g, last=(l == depth - 1))
    return x2d.reshape(batch, seq, d)
```

```python
import functools
import math

import jax
import jax.numpy as jnp
from jax import lax
from jax.experimental import pallas as pl
from jax.experimental.pallas import tpu as pltpu

F32 = jnp.float32
BF16 = jnp.bfloat16

D_MODEL = 1024
POOL_WINDOWS = (2, 4, 8, 16)
POOL_WIDTH = 256
POOL_GROUP = 64
DIFF_WIDTH = 512
DIFF_HEAD_DIM = 64
DIFF_HEADS = 4
SB_WIDTH = 256
SB_HEAD_DIM = 64
SB_HEADS = 4
D_IN = POOL_WIDTH + 3 * DIFF_WIDTH + 3 * SB_WIDTH
D_FF = 2816
EPS = 1e-6

LANES = 128
N_SLABS = (D_IN - POOL_WIDTH) // LANES
SB_SLABS = SB_WIDTH // LANES
BLK_DQ, BLK_DK = 0, 1
SLAB_DV = 2 * DIFF_HEADS
BLK_SQ, BLK_SK, BLK_SV = 6, 7, 8

TM_PROJ = 512
TM_FFN = 512
TP_POOL = 512
POOL_HALO = 16
TQ = 256
TK = 256
FF_CHUNK = 512
VMEM_LIMIT = 56 * 1024 * 1024

_NT = (((1,), (1,)), ((), ()))


def _resident(shape):
    return pl.BlockSpec(shape, lambda *_: (0,) * len(shape), pipeline_mode=pl.Buffered(1))


def _rms(x, g):
    ms = jnp.mean(x * x, axis=-1, keepdims=True)
    return x * lax.rsqrt(ms + EPS) * g


def _proj_kernel(x_ref, g_ref, w_ref, u_ref, p_ref, vt_ref):
    h = _rms(x_ref[...], g_ref[...]).astype(BF16)
    u_ref[...] = jnp.dot(h, w_ref[:, :POOL_WIDTH], preferred_element_type=F32)
    for c0 in range(0, N_SLABS, 4):
        lo = POOL_WIDTH + c0 * LANES
        n = min(4, N_SLABS - c0)
        acc = jnp.dot(h, w_ref[:, lo:lo + n * LANES], preferred_element_type=F32)
        for j in range(n):
            slab = acc[:, j * LANES:(j + 1) * LANES]
            p_ref[c0 + j] = slab.astype(BF16)
            head = c0 + j - SLAB_DV
            if 0 <= head < DIFF_HEADS:
                for c in range(TM_PROJ // TK):
                    vt_ref[head, c] = slab[c * TK:(c + 1) * TK, :].T.astype(BF16)


def _proj(x2d, g, w_bf16):
    t = x2d.shape[0]
    return pl.pallas_call(
        _proj_kernel,
        grid=(t // TM_PROJ,),
        in_specs=[
            pl.BlockSpec((TM_PROJ, D_MODEL), lambda i: (i, 0)),
            _resident((1, D_MODEL)),
            _resident((D_MODEL, D_IN)),
        ],
        out_specs=[
            pl.BlockSpec((TM_PROJ, POOL_WIDTH), lambda i: (i, 0)),
            pl.BlockSpec((N_SLABS, TM_PROJ, LANES), lambda i: (0, i, 0)),
            pl.BlockSpec((DIFF_HEADS, TM_PROJ // TK, LANES, TK), lambda i: (0, i, 0, 0)),
        ],
        out_shape=[
            jax.ShapeDtypeStruct((t, POOL_WIDTH), F32),
            jax.ShapeDtypeStruct((N_SLABS, t, LANES), BF16),
            jax.ShapeDtypeStruct((DIFF_HEADS, t // TK, LANES, TK), BF16),
        ],
        compiler_params=pltpu.CompilerParams(
            dimension_semantics=("arbitrary",), vmem_limit_bytes=VMEM_LIMIT),
        name="proj",
    )(x2d, g.reshape(1, D_MODEL), w_bf16)


def _pool_kernel(u_ref, halo_ref, w_ref, scale_ref, o_ref):
    i = pl.program_id(1)
    halo = jnp.where(i > 0, halo_ref[...], 0.0)
    e0 = jnp.concatenate([halo, u_ref[...]], axis=0)
    s2 = e0 + pltpu.roll(e0, 1, 0)
    s4 = s2 + pltpu.roll(s2, 2, 0)
    s8 = s4 + pltpu.roll(s4, 4, 0)
    s16 = s8 + pltpu.roll(s8, 8, 0)
    shape = e0.shape
    lane = lax.broadcasted_iota(jnp.int32, shape, 1)
    row = lax.broadcasted_iota(jnp.int32, shape, 0)
    g0, g1, g2 = lane < POOL_GROUP, lane < 2 * POOL_GROUP, lane < 3 * POOL_GROUP
    win = jnp.where(g0, s2, jnp.where(g1, s4, jnp.where(g2, s8, s16)))
    width = jnp.where(g0, POOL_WINDOWS[0],
                      jnp.where(g1, POOL_WINDOWS[1],
                                jnp.where(g2, POOL_WINDOWS[2], POOL_WINDOWS[3])))
    pos = i * TP_POOL + row - POOL_HALO
    cnt = jnp.maximum(jnp.minimum(pos + 1, width), 1).astype(F32)
    pooled = (win / cnt - e0)[POOL_HALO:]
    mixed = jnp.dot(pooled.astype(BF16), w_ref[...], preferred_element_type=F32)
    o_ref[...] = (mixed * scale_ref[...]).astype(BF16)


def _pool(u, w_blockdiag, scale, batch, seq):
    t = u.shape[0]
    nblk = seq // TP_POOL
    halo_per_blk = TP_POOL // POOL_HALO
    return pl.pallas_call(
        _pool_kernel,
        grid=(batch, nblk),
        in_specs=[
            pl.BlockSpec((TP_POOL, POOL_WIDTH), lambda b, i: (b * nblk + i, 0)),
            pl.BlockSpec((POOL_HALO, POOL_WIDTH),
                         lambda b, i: (jnp.maximum((b * nblk + i) * halo_per_blk - 1, 0), 0)),
            _resident((POOL_WIDTH, POOL_WIDTH)),
            _resident((1, POOL_WIDTH)),
        ],
        out_specs=pl.BlockSpec((TP_POOL, POOL_WIDTH), lambda b, i: (b * nblk + i, 0)),
        out_shape=jax.ShapeDtypeStruct((t, POOL_WIDTH), BF16),
        compiler_params=pltpu.CompilerParams(
            dimension_semantics=("arbitrary", "arbitrary"), vmem_limit_bytes=VMEM_LIMIT),
        name="pool",
    )(u, u, w_blockdiag, scale.reshape(1, POOL_WIDTH))


def _twice(a):
    return jnp.concatenate([a] * (TK // LANES), axis=1)


def _diff_kernel(q_ref, k_ref, vt_ref, lamp_ref, g_ref, o_ref,
                 m_ref, l_ref, acc_ref, s0_ref, s1_ref, *, lam_init):
    qi = pl.program_id(1)
    scale = jnp.asarray(DIFF_HEAD_DIM ** -0.5, BF16)
    lane = lax.broadcasted_iota(jnp.int32, (TQ, LANES), 1)
    first_map = lane < DIFF_HEAD_DIM
    qms = []
    for h in range(DIFF_HEADS):
        q = q_ref[h] * scale
        zero = jnp.zeros_like(q)
        qms += [jnp.where(first_map, q, zero), jnp.where(first_map, zero, q)]
    key = lax.broadcasted_iota(jnp.int32, (TK, TQ), 0)
    qry = lax.broadcasted_iota(jnp.int32, (TK, TQ), 1)
    causal = key <= qry

    s_refs = (s0_ref, s1_ref)

    def score_tile(kb, slot):
        start = pl.multiple_of(kb * TK, TK)
        for i, qm in enumerate(qms):
            s_refs[slot][i] = lax.dot_general(k_ref[i // 2, pl.ds(start, TK), :], qm, _NT,
                                              preferred_element_type=F32)

    def softmax_tile(kb, slot, diagonal):
        probs, alphas = [], []
        for i in range(len(qms)):
            s = s_refs[slot][i]
            if diagonal:
                s = jnp.where(causal, s, -jnp.inf)
            m_prev = m_ref[i]
            m_new = jnp.maximum(m_prev, jnp.max(s, axis=0, keepdims=True))
            alphas.append(jnp.exp(m_prev - m_new))
            p = jnp.exp(s - m_new)
            l_ref[i] = alphas[i] * l_ref[i] + jnp.sum(p, axis=0, keepdims=True)
            m_ref[i] = m_new
            probs.append(p.astype(BF16))
        for i, p in enumerate(probs):
            pv = jnp.dot(vt_ref[i // 2, kb], p, preferred_element_type=F32)
            acc_ref[i] = alphas[i] * acc_ref[i] + pv

    m_ref[...] = jnp.full(m_ref.shape, -jnp.inf, F32)
    l_ref[...] = jnp.zeros(l_ref.shape, F32)
    acc_ref[...] = jnp.zeros(acc_ref.shape, F32)
    score_tile(0, 0)

    def step(kb, slot):
        score_tile(kb + 1, 1 - slot)
        softmax_tile(kb, slot, False)

    def body(kb, carry):
        for slot in range(2):
            pl.when(kb % 2 == slot)(functools.partial(step, kb, slot))
        return carry

    lax.fori_loop(0, qi, body, 0)
    for slot in range(2):
        pl.when(qi % 2 == slot)(functools.partial(softmax_tile, qi, slot, True))

    lamp = lamp_ref[...]
    lam = (jnp.exp(jnp.sum(lamp[0:1] * lamp[1:2], axis=1, keepdims=True))
           - jnp.exp(jnp.sum(lamp[2:3] * lamp[3:4], axis=1, keepdims=True)) + lam_init)
    for h in range(DIFF_HEADS):
        o = acc_ref[2 * h] / l_ref[2 * h] - lam * (acc_ref[2 * h + 1] / l_ref[2 * h + 1])
        ms = jnp.mean(o * o, axis=0, keepdims=True)
        y = o * lax.rsqrt(ms + EPS) * g_ref[...] * (1.0 - lam_init)
        o_ref[:, h * LANES:(h + 1) * LANES] = y.T.astype(BF16)


def _diff_attention(p3, vt, lamp, g, lam_init, batch, seq):
    t = p3.shape[1]
    nq = seq // TQ
    nk = seq // TK
    vdim = 2 * DIFF_HEAD_DIM
    return pl.pallas_call(
        functools.partial(_diff_kernel, lam_init=lam_init),
        grid=(batch, nq),
        in_specs=[
            pl.BlockSpec((DIFF_HEADS, TQ, LANES), lambda b, i: (BLK_DQ, b * nq + i, 0)),
            pl.BlockSpec((DIFF_HEADS, seq, LANES), lambda b, i: (BLK_DK, b, 0)),
            pl.BlockSpec((DIFF_HEADS, nk, vdim, TK), lambda b, i: (0, b, 0, 0)),
            _resident((4, DIFF_HEAD_DIM)),
            _resident((vdim, 1)),
        ],
        out_specs=pl.BlockSpec((TQ, DIFF_WIDTH), lambda b, i: (b * nq + i, 0)),
        out_shape=jax.ShapeDtypeStruct((t, DIFF_WIDTH), BF16),
        scratch_shapes=[
            pltpu.VMEM((2 * DIFF_HEADS, 1, TQ), F32),
            pltpu.VMEM((2 * DIFF_HEADS, 1, TQ), F32),
            pltpu.VMEM((2 * DIFF_HEADS, vdim, TQ), F32),
            pltpu.VMEM((2 * DIFF_HEADS, TK, TQ), F32),
            pltpu.VMEM((2 * DIFF_HEADS, TK, TQ), F32),
        ],
        compiler_params=pltpu.CompilerParams(
            dimension_semantics=("arbitrary",) * 2, vmem_limit_bytes=VMEM_LIMIT),
        name="diff_attn",
    )(p3, p3, vt, lamp, g.reshape(vdim, 1))


def _sb_kernel(q_ref, k_ref, v_ref, g_ref, o_ref, tri_ref, carry_ref, acc_ref, z0_ref, z1_ref):
    b, qi = pl.program_id(0), pl.program_id(1)

    @pl.when((b == 0) & (qi == 0))
    def _():
        j = lax.broadcasted_iota(jnp.int32, (2 * TK, TK), 0)
        s = lax.broadcasted_iota(jnp.int32, (2 * TK, TK), 1)
        j = jnp.where(j >= TK, j - TK, j)
        tri_ref[...] = jnp.where(j >= s, 1.0, 0.0).astype(BF16)

    scale = jnp.asarray(SB_HEAD_DIM ** -0.5, BF16)
    lane = lax.broadcasted_iota(jnp.int32, (TQ, LANES), 1)
    lower = lane < SB_HEAD_DIM
    qms = []
    for sl in range(SB_SLABS):
        q = q_ref[sl] * scale
        zero = jnp.zeros_like(q)
        qms += [jnp.where(lower, q, zero), jnp.where(lower, zero, q)]
    row = lax.broadcasted_iota(jnp.int32, (TQ, TK), 0)
    col = lax.broadcasted_iota(jnp.int32, (TQ, TK), 1)
    strict = col < row

    z_refs = (z0_ref, z1_ref)

    def score_tile(kb, slot):
        start = pl.multiple_of(kb * TK, TK)
        for hd, qm in enumerate(qms):
            z_refs[slot][hd] = lax.dot_general(qm, k_ref[hd // 2, pl.ds(start, TK), :], _NT,
                                               preferred_element_type=F32)

    def weight_tile(kb, slot, diagonal):
        start = pl.multiple_of(kb * TK, TK)
        zs = [z_refs[slot][hd] for hd in range(SB_HEADS)]
        splits = []
        for z in zs:
            sp = jnp.maximum(z, 0.0) + jnp.log(1.0 + jnp.exp(-jnp.abs(z)))
            if diagonal:
                sp = jnp.where(strict, sp, 0.0)
            hi = sp.astype(BF16)
            lo = (sp - hi.astype(F32)).astype(BF16)
            splits.append(jnp.concatenate([hi, lo], axis=1))
        sums = [jnp.dot(s, tri_ref[...], preferred_element_type=F32) for s in splits]
        weights = []
        for hd, (z, c) in enumerate(zip(zs, sums)):
            if not diagonal:
                c = c + _twice(carry_ref[hd])
            a = jnp.exp(z - c)
            if diagonal:
                a = jnp.where(strict, a, 0.0)
            carry_ref[hd] = jnp.broadcast_to(c[:, 0:1], (TQ, LANES))
            weights.append(a.astype(BF16))
        for hd, a in enumerate(weights):
            pv = jnp.dot(a, v_ref[hd // 2, pl.ds(start, TK), :], preferred_element_type=F32)
            acc_ref[hd] = pv if diagonal else acc_ref[hd] + pv

    score_tile(qi, 0)
    score_tile(jnp.maximum(qi - 1, 0), 1)
    weight_tile(qi, 0, True)

    def step(j, slot):
        kb = qi - 1 - j
        score_tile(jnp.maximum(kb - 1, 0), 1 - slot)
        weight_tile(kb, slot, False)

    def body(j, carry):
        for slot in range(2):
            pl.when((j + 1) % 2 == slot)(functools.partial(step, j, slot))
        return carry

    lax.fori_loop(0, qi, body, 0)

    lane_o = lax.broadcasted_iota(jnp.int32, (TQ, LANES), 1)
    first = lane_o < SB_HEAD_DIM
    for sl in range(SB_SLABS):
        o = jnp.where(first, acc_ref[2 * sl], acc_ref[2 * sl + 1])
        sq = o * o
        ms = jnp.where(first,
                       jnp.sum(jnp.where(first, sq, 0.0), axis=1, keepdims=True),
                       jnp.sum(jnp.where(first, 0.0, sq), axis=1, keepdims=True)) / SB_HEAD_DIM
        o_ref[:, sl * LANES:(sl + 1) * LANES] = (
            o * lax.rsqrt(ms + EPS) * g_ref[...]).astype(BF16)


def _sb_attention(p3, g, batch, seq):
    t = p3.shape[1]
    nq = seq // TQ
    return pl.pallas_call(
        _sb_kernel,
        grid=(batch, nq),
        in_specs=[
            pl.BlockSpec((SB_SLABS, TQ, LANES), lambda b, i: (BLK_SQ, b * nq + i, 0)),
            pl.BlockSpec((SB_SLABS, seq, LANES), lambda b, i: (BLK_SK, b, 0)),
            pl.BlockSpec((SB_SLABS, seq, LANES), lambda b, i: (BLK_SV, b, 0)),
            _resident((1, LANES)),
        ],
        out_specs=pl.BlockSpec((TQ, SB_WIDTH), lambda b, i: (b * nq + i, 0)),
        out_shape=jax.ShapeDtypeStruct((t, SB_WIDTH), BF16),
        scratch_shapes=[
            pltpu.VMEM((2 * TK, TK), BF16),
            pltpu.VMEM((SB_HEADS, TQ, LANES), F32),
            pltpu.VMEM((SB_HEADS, TQ, LANES), F32),
            pltpu.VMEM((SB_HEADS, TQ, TK), F32),
            pltpu.VMEM((SB_HEADS, TQ, TK), F32),
        ],
        compiler_params=pltpu.CompilerParams(
            dimension_semantics=("arbitrary",) * 2, vmem_limit_bytes=VMEM_LIMIT),
        name="sb_attn",
    )(p3, p3, p3, jnp.concatenate([g, g]).reshape(1, LANES))


def _ffn_kernel(x_ref, a_ref, b_ref, c_ref, wo_ref, g2_ref, wg_ref, wu_ref, wd_ref, gf_ref,
                o_ref, x1_ref, act_ref, *, last):
    mix = jnp.concatenate([a_ref[...], b_ref[...], c_ref[...]], axis=1)
    x1_ref[...] = x_ref[...] + jnp.dot(mix, wo_ref[...], preferred_element_type=F32)
    h2 = _rms(x1_ref[...], g2_ref[...]).astype(BF16)
    for f0 in range(0, D_FF, FF_CHUNK):
        f1 = min(f0 + FF_CHUNK, D_FF)
        gate = jnp.dot(h2, wg_ref[:, f0:f1], preferred_element_type=F32)
        up = jnp.dot(h2, wu_ref[:, f0:f1], preferred_element_type=F32)
        act_ref[:, f0:f1] = (gate * (1.0 / (1.0 + jnp.exp(-gate))) * up).astype(BF16)
    x2 = x1_ref[...] + jnp.dot(act_ref[...], wd_ref[...], preferred_element_type=F32)
    o_ref[...] = _rms(x2, gf_ref[...]) if last else x2


def _ffn(x2d, a, b, c, wo, g2, wg, wu, wd, gf, last):
    t = x2d.shape[0]
    rows = lambda w: pl.BlockSpec((TM_FFN, w), lambda i: (i, 0))
    return pl.pallas_call(
        functools.partial(_ffn_kernel, last=last),
        grid=(t // TM_FFN,),
        in_specs=[
            rows(D_MODEL), rows(POOL_WIDTH), rows(DIFF_WIDTH), rows(SB_WIDTH),
            _resident((D_MODEL, D_MODEL)), _resident((1, D_MODEL)),
            _resident((D_MODEL, D_FF)), _resident((D_MODEL, D_FF)), _resident((D_FF, D_MODEL)),
            _resident((1, D_MODEL)),
        ],
        out_specs=rows(D_MODEL),
        out_shape=jax.ShapeDtypeStruct((t, D_MODEL), F32),
        scratch_shapes=[
            pltpu.VMEM((TM_FFN, D_MODEL), F32),
            pltpu.VMEM((TM_FFN, D_FF), BF16),
        ],
        compiler_params=pltpu.CompilerParams(
            dimension_semantics=("arbitrary",), vmem_limit_bytes=VMEM_LIMIT),
        name="ffn_last" if last else "ffn",
    )(x2d, a, b, c, wo, g2.reshape(1, D_MODEL), wg, wu, wd, gf.reshape(1, D_MODEL))


def kernel(x, norm1_g, w_in, pool_w, pool_scale, lam_q1, lam_k1, lam_q2, lam_k2,
           diff_norm_g, sb_norm_g, w_out, norm2_g, w_gate, w_up, w_down, final_norm_g):
    batch, seq, d = x.shape
    depth = w_in.shape[0]
    assert d == D_MODEL and seq % TQ == 0 and seq % TP_POOL == 0
    assert (batch * seq) % TM_PROJ == 0 and (batch * seq) % TM_FFN == 0

    x2d = x.reshape(batch * seq, d)
    for l in range(depth):
        u, p3, vt = _proj(x2d, norm1_g[l], w_in[l].astype(BF16))
        w_bd = jax.scipy.linalg.block_diag(*[pool_w[l, gi] for gi in range(len(POOL_WINDOWS))])
        a = _pool(u, w_bd.astype(BF16), pool_scale[l], batch, seq)
        lam_init = 0.8 - 0.6 * math.exp(-0.3 * l)
        lamp = jnp.stack([lam_q1[l], lam_k1[l], lam_q2[l], lam_k2[l]])
        bo = _diff_attention(p3, vt, lamp, diff_norm_g[l], lam_init, batch, seq)
        co = _sb_attention(p3, sb_norm_g[l], batch, seq)
        x2d = _ffn(x2d, a, bo, co, w_out[l].astype(BF16), norm2_g[l],
                   w_gate[l].astype(BF16), w_up[l].astype(BF16), w_down[l].astype(BF16),
                   final_norm_g, last=(l == depth - 1))
    return x2d.reshape(batch, seq, d)
```

```python
import functools
import math

import jax
import jax.numpy as jnp
from jax import lax
from jax.experimental import pallas as pl
from jax.experimental.pallas import tpu as pltpu

F32 = jnp.float32
BF16 = jnp.bfloat16

D_MODEL = 1024
POOL_WINDOWS = (2, 4, 8, 16)
POOL_WIDTH = 256
POOL_GROUP = 64
DIFF_WIDTH = 512
DIFF_HEAD_DIM = 64
DIFF_HEADS = 4
SB_WIDTH = 256
SB_HEAD_DIM = 64
SB_HEADS = 4
D_IN = POOL_WIDTH + 3 * DIFF_WIDTH + 3 * SB_WIDTH
D_FF = 2816
EPS = 1e-6

LANES = 128
N_SLABS = (D_IN - POOL_WIDTH) // LANES
SB_SLABS = SB_WIDTH // LANES
BLK_DQ, BLK_DK = 0, 1
SLAB_DV = 2 * DIFF_HEADS
BLK_SQ, BLK_SK, BLK_SV = 6, 7, 8

TM_PROJ = 512
TM_FFN = 512
TP_POOL = 512
POOL_HALO = 16
TQ = 256
TK = 256
FF_CHUNK = 512
VMEM_LIMIT = 56 * 1024 * 1024
SB_DEAD_MASS = 105.0

_NT = (((1,), (1,)), ((), ()))


def _resident(shape):
    return pl.BlockSpec(shape, lambda *_: (0,) * len(shape), pipeline_mode=pl.Buffered(1))


def _layer_weight(shape, layer):
    return pl.BlockSpec((None,) + shape, lambda *_: (layer, 0, 0), pipeline_mode=pl.Buffered(1))


def _rms(x, g):
    ms = jnp.mean(x * x, axis=-1, keepdims=True)
    return x * lax.rsqrt(ms + EPS) * g


def _proj_kernel(x_ref, g_ref, w_ref, u_ref, p_ref, vt_ref):
    h = _rms(x_ref[...], g_ref[...]).astype(BF16)
    u_ref[...] = jnp.dot(h, w_ref[:, :POOL_WIDTH], preferred_element_type=F32)
    for c0 in range(0, N_SLABS, 4):
        lo = POOL_WIDTH + c0 * LANES
        n = min(4, N_SLABS - c0)
        acc = jnp.dot(h, w_ref[:, lo:lo + n * LANES], preferred_element_type=F32)
        for j in range(n):
            slab = acc[:, j * LANES:(j + 1) * LANES]
            p_ref[c0 + j] = slab.astype(BF16)
            head = c0 + j - SLAB_DV
            if 0 <= head < DIFF_HEADS:
                for c in range(TM_PROJ // TK):
                    vt_ref[head, c] = slab[c * TK:(c + 1) * TK, :].T.astype(BF16)


def _proj(x2d, g, w_stack, layer):
    t = x2d.shape[0]
    return pl.pallas_call(
        _proj_kernel,
        grid=(t // TM_PROJ,),
        in_specs=[
            pl.BlockSpec((TM_PROJ, D_MODEL), lambda i: (i, 0)),
            _resident((1, D_MODEL)),
            _layer_weight((D_MODEL, D_IN), layer),
        ],
        out_specs=[
            pl.BlockSpec((TM_PROJ, POOL_WIDTH), lambda i: (i, 0)),
            pl.BlockSpec((N_SLABS, TM_PROJ, LANES), lambda i: (0, i, 0)),
            pl.BlockSpec((DIFF_HEADS, TM_PROJ // TK, LANES, TK), lambda i: (0, i, 0, 0)),
        ],
        out_shape=[
            jax.ShapeDtypeStruct((t, POOL_WIDTH), F32),
            jax.ShapeDtypeStruct((N_SLABS, t, LANES), BF16),
            jax.ShapeDtypeStruct((DIFF_HEADS, t // TK, LANES, TK), BF16),
        ],
        compiler_params=pltpu.CompilerParams(
            dimension_semantics=("arbitrary",), vmem_limit_bytes=VMEM_LIMIT),
        name="proj",
    )(x2d, g.reshape(1, D_MODEL), w_stack)


def _pool_kernel(u_ref, halo_ref, w_ref, scale_ref, o_ref):
    i = pl.program_id(1)
    halo = jnp.where(i > 0, halo_ref[...], 0.0)
    e0 = jnp.concatenate([halo, u_ref[...]], axis=0)
    s2 = e0 + pltpu.roll(e0, 1, 0)
    s4 = s2 + pltpu.roll(s2, 2, 0)
    s8 = s4 + pltpu.roll(s4, 4, 0)
    s16 = s8 + pltpu.roll(s8, 8, 0)
    shape = e0.shape
    lane = lax.broadcasted_iota(jnp.int32, shape, 1)
    row = lax.broadcasted_iota(jnp.int32, shape, 0)
    g0, g1, g2 = lane < POOL_GROUP, lane < 2 * POOL_GROUP, lane < 3 * POOL_GROUP
    win = jnp.where(g0, s2, jnp.where(g1, s4, jnp.where(g2, s8, s16)))
    width = jnp.where(g0, POOL_WINDOWS[0],
                      jnp.where(g1, POOL_WINDOWS[1],
                                jnp.where(g2, POOL_WINDOWS[2], POOL_WINDOWS[3])))
    pos = i * TP_POOL + row - POOL_HALO
    cnt = jnp.maximum(jnp.minimum(pos + 1, width), 1).astype(F32)
    pooled = (win / cnt - e0)[POOL_HALO:]
    mixed = jnp.dot(pooled.astype(BF16), w_ref[...], preferred_element_type=F32)
    o_ref[...] = (mixed * scale_ref[...]).astype(BF16)


def _pool(u, w_blockdiag, scale, batch, seq):
    t = u.shape[0]
    nblk = seq // TP_POOL
    halo_per_blk = TP_POOL // POOL_HALO
    return pl.pallas_call(
        _pool_kernel,
        grid=(batch, nblk),
        in_specs=[
            pl.BlockSpec((TP_POOL, POOL_WIDTH), lambda b, i: (b * nblk + i, 0)),
            pl.BlockSpec((POOL_HALO, POOL_WIDTH),
                         lambda b, i: (jnp.maximum((b * nblk + i) * halo_per_blk - 1, 0), 0)),
            _resident((POOL_WIDTH, POOL_WIDTH)),
            _resident((1, POOL_WIDTH)),
        ],
        out_specs=pl.BlockSpec((TP_POOL, POOL_WIDTH), lambda b, i: (b * nblk + i, 0)),
        out_shape=jax.ShapeDtypeStruct((t, POOL_WIDTH), BF16),
        compiler_params=pltpu.CompilerParams(
            dimension_semantics=("arbitrary", "arbitrary"), vmem_limit_bytes=VMEM_LIMIT),
        name="pool",
    )(u, u, w_blockdiag, scale.reshape(1, POOL_WIDTH))


def _twice(a):
    return jnp.concatenate([a] * (TK // LANES), axis=1)


def _diff_kernel(q_ref, k_ref, vt_ref, lamp_ref, g_ref, o_ref,
                 m_ref, l_ref, acc_ref, s0_ref, s1_ref, *, lam_init):
    qi = pl.program_id(1)
    scale = jnp.asarray(DIFF_HEAD_DIM ** -0.5, BF16)
    lane = lax.broadcasted_iota(jnp.int32, (TQ, LANES), 1)
    first_map = lane < DIFF_HEAD_DIM
    qms = []
    for h in range(DIFF_HEADS):
        q = q_ref[h] * scale
        zero = jnp.zeros_like(q)
        qms += [jnp.where(first_map, q, zero), jnp.where(first_map, zero, q)]
    key = lax.broadcasted_iota(jnp.int32, (TK, TQ), 0)
    qry = lax.broadcasted_iota(jnp.int32, (TK, TQ), 1)
    causal = key <= qry

    s_refs = (s0_ref, s1_ref)

    def score_tile(kb, slot):
        start = pl.multiple_of(kb * TK, TK)
        for i, qm in enumerate(qms):
            s_refs[slot][i] = lax.dot_general(k_ref[i // 2, pl.ds(start, TK), :], qm, _NT,
                                              preferred_element_type=F32)

    def softmax_tile(kb, slot, diagonal):
        probs, alphas = [], []
        for i in range(len(qms)):
            s = s_refs[slot][i]
            if diagonal:
                s = jnp.where(causal, s, -jnp.inf)
            m_prev = m_ref[i]
            m_new = jnp.maximum(m_prev, jnp.max(s, axis=0, keepdims=True))
            alphas.append(jnp.exp(m_prev - m_new))
            p = jnp.exp(s - m_new)
            l_ref[i] = alphas[i] * l_ref[i] + jnp.sum(p, axis=0, keepdims=True)
            m_ref[i] = m_new
            probs.append(p.astype(BF16))
        for i, p in enumerate(probs):
            pv = jnp.dot(vt_ref[i // 2, kb], p, preferred_element_type=F32)
            acc_ref[i] = alphas[i] * acc_ref[i] + pv

    m_ref[...] = jnp.full(m_ref.shape, -jnp.inf, F32)
    l_ref[...] = jnp.zeros(l_ref.shape, F32)
    acc_ref[...] = jnp.zeros(acc_ref.shape, F32)
    score_tile(0, 0)

    def step(kb, slot):
        score_tile(kb + 1, 1 - slot)
        softmax_tile(kb, slot, False)

    def body(kb, carry):
        for slot in range(2):
            pl.when(kb % 2 == slot)(functools.partial(step, kb, slot))
        return carry

    lax.fori_loop(0, qi, body, 0)
    for slot in range(2):
        pl.when(qi % 2 == slot)(functools.partial(softmax_tile, qi, slot, True))

    lamp = lamp_ref[...]
    lam = (jnp.exp(jnp.sum(lamp[0:1] * lamp[1:2], axis=1, keepdims=True))
           - jnp.exp(jnp.sum(lamp[2:3] * lamp[3:4], axis=1, keepdims=True)) + lam_init)
    for h in range(DIFF_HEADS):
        o = acc_ref[2 * h] / l_ref[2 * h] - lam * (acc_ref[2 * h + 1] / l_ref[2 * h + 1])
        ms = jnp.mean(o * o, axis=0, keepdims=True)
        y = o * lax.rsqrt(ms + EPS) * g_ref[...] * (1.0 - lam_init)
        o_ref[:, h * LANES:(h + 1) * LANES] = y.T.astype(BF16)


def _diff_attention(p3, vt, lamp, g, lam_init, batch, seq):
    t = p3.shape[1]
    nq = seq // TQ
    nk = seq // TK
    vdim = 2 * DIFF_HEAD_DIM
    return pl.pallas_call(
        functools.partial(_diff_kernel, lam_init=lam_init),
        grid=(batch, nq),
        in_specs=[
            pl.BlockSpec((DIFF_HEADS, TQ, LANES), lambda b, i: (BLK_DQ, b * nq + i, 0)),
            pl.BlockSpec((DIFF_HEADS, seq, LANES), lambda b, i: (BLK_DK, b, 0)),
            pl.BlockSpec((DIFF_HEADS, nk, vdim, TK), lambda b, i: (0, b, 0, 0)),
            _resident((4, DIFF_HEAD_DIM)),
            _resident((vdim, 1)),
        ],
        out_specs=pl.BlockSpec((TQ, DIFF_WIDTH), lambda b, i: (b * nq + i, 0)),
        out_shape=jax.ShapeDtypeStruct((t, DIFF_WIDTH), BF16),
        scratch_shapes=[
            pltpu.VMEM((2 * DIFF_HEADS, 1, TQ), F32),
            pltpu.VMEM((2 * DIFF_HEADS, 1, TQ), F32),
            pltpu.VMEM((2 * DIFF_HEADS, vdim, TQ), F32),
            pltpu.VMEM((2 * DIFF_HEADS, TK, TQ), F32),
            pltpu.VMEM((2 * DIFF_HEADS, TK, TQ), F32),
        ],
        compiler_params=pltpu.CompilerParams(
            dimension_semantics=("arbitrary",) * 2, vmem_limit_bytes=VMEM_LIMIT),
        name="diff_attn",
    )(p3, p3, vt, lamp, g.reshape(vdim, 1))


def _sb_kernel(q_ref, k_ref, v_ref, g_ref, o_ref, tri_ref, carry_ref, acc_ref, z0_ref, z1_ref):
    b, qi = pl.program_id(0), pl.program_id(1)

    @pl.when((b == 0) & (qi == 0))
    def _():
        j = lax.broadcasted_iota(jnp.int32, (2 * TK, TK), 0)
        s = lax.broadcasted_iota(jnp.int32, (2 * TK, TK), 1)
        j = jnp.where(j >= TK, j - TK, j)
        tri_ref[...] = jnp.where(j >= s, 1.0, 0.0).astype(BF16)

    scale = jnp.asarray(SB_HEAD_DIM ** -0.5, BF16)
    lane = lax.broadcasted_iota(jnp.int32, (TQ, LANES), 1)
    lower = lane < SB_HEAD_DIM
    qms = []
    for sl in range(SB_SLABS):
        q = q_ref[sl] * scale
        zero = jnp.zeros_like(q)
        qms += [jnp.where(lower, q, zero), jnp.where(lower, zero, q)]
    row = lax.broadcasted_iota(jnp.int32, (TQ, TK), 0)
    col = lax.broadcasted_iota(jnp.int32, (TQ, TK), 1)
    strict = col < row

    z_refs = (z0_ref, z1_ref)

    def score_tile(kb, slot):
        start = pl.multiple_of(kb * TK, TK)
        for hd, qm in enumerate(qms):
            z_refs[slot][hd] = lax.dot_general(qm, k_ref[hd // 2, pl.ds(start, TK), :], _NT,
                                               preferred_element_type=F32)

    def weight_tile(kb, slot, diagonal):
        start = pl.multiple_of(kb * TK, TK)
        zs = [z_refs[slot][hd] for hd in range(SB_HEADS)]
        splits = []
        for z in zs:
            sp = jnp.maximum(z, 0.0) + jnp.log(1.0 + jnp.exp(-jnp.abs(z)))
            if diagonal:
                sp = jnp.where(strict, sp, 0.0)
            hi = sp.astype(BF16)
            lo = (sp - hi.astype(F32)).astype(BF16)
            splits.append(jnp.concatenate([hi, lo], axis=1))
        sums = [jnp.dot(s, tri_ref[...], preferred_element_type=F32) for s in splits]
        weights = []
        for hd, (z, c) in enumerate(zip(zs, sums)):
            if not diagonal:
                c = c + _twice(carry_ref[hd])
            a = jnp.exp(z - c)
            if diagonal:
                a = jnp.where(strict, a, 0.0)
            carry_ref[hd] = jnp.broadcast_to(c[:, 0:1], (TQ, LANES))
            weights.append(a.astype(BF16))
        for hd, a in enumerate(weights):
            pv = jnp.dot(a, v_ref[hd // 2, pl.ds(start, TK), :], preferred_element_type=F32)
            acc_ref[hd] = pv if diagonal else acc_ref[hd] + pv

    score_tile(qi, 0)
    score_tile(jnp.maximum(qi - 1, 0), 1)
    weight_tile(qi, 0, True)

    def step(j, slot):
        kb = qi - 1 - j
        score_tile(jnp.maximum(kb - 1, 0), 1 - slot)
        weight_tile(kb, slot, False)

    def open_rows():
        return (jnp.min(carry_ref[...]) < SB_DEAD_MASS).astype(jnp.int32)

    def body(state):
        j, _ = state
        for slot in range(2):
            pl.when((j + 1) % 2 == slot)(functools.partial(step, j, slot))
        return j + 1, open_rows()

    lax.while_loop(lambda state: (state[0] < qi) & (state[1] > 0), body,
                   (jnp.int32(0), open_rows()))

    lane_o = lax.broadcasted_iota(jnp.int32, (TQ, LANES), 1)
    first = lane_o < SB_HEAD_DIM
    for sl in range(SB_SLABS):
        o = jnp.where(first, acc_ref[2 * sl], acc_ref[2 * sl + 1])
        sq = o * o
        ms = jnp.where(first,
                       jnp.sum(jnp.where(first, sq, 0.0), axis=1, keepdims=True),
                       jnp.sum(jnp.where(first, 0.0, sq), axis=1, keepdims=True)) / SB_HEAD_DIM
        o_ref[:, sl * LANES:(sl + 1) * LANES] = (
            o * lax.rsqrt(ms + EPS) * g_ref[...]).astype(BF16)


def _sb_attention(p3, g, batch, seq):
    t = p3.shape[1]
    nq = seq // TQ
    return pl.pallas_call(
        _sb_kernel,
        grid=(batch, nq),
        in_specs=[
            pl.BlockSpec((SB_SLABS, TQ, LANES), lambda b, i: (BLK_SQ, b * nq + i, 0)),
            pl.BlockSpec((SB_SLABS, seq, LANES), lambda b, i: (BLK_SK, b, 0)),
            pl.BlockSpec((SB_SLABS, seq, LANES), lambda b, i: (BLK_SV, b, 0)),
            _resident((1, LANES)),
        ],
        out_specs=pl.BlockSpec((TQ, SB_WIDTH), lambda b, i: (b * nq + i, 0)),
        out_shape=jax.ShapeDtypeStruct((t, SB_WIDTH), BF16),
        scratch_shapes=[
            pltpu.VMEM((2 * TK, TK), BF16),
            pltpu.VMEM((SB_HEADS, TQ, LANES), F32),
            pltpu.VMEM((SB_HEADS, TQ, LANES), F32),
            pltpu.VMEM((SB_HEADS, TQ, TK), F32),
            pltpu.VMEM((SB_HEADS, TQ, TK), F32),
        ],
        compiler_params=pltpu.CompilerParams(
            dimension_semantics=("arbitrary",) * 2, vmem_limit_bytes=VMEM_LIMIT),
        name="sb_attn",
    )(p3, p3, p3, jnp.concatenate([g, g]).reshape(1, LANES))


def _ffn_kernel(x_ref, a_ref, b_ref, c_ref, wo_ref, g2_ref, wg_ref, wu_ref, wd_ref, gf_ref,
                o_ref, x1_ref, act_ref, *, last):
    mix = jnp.concatenate([a_ref[...], b_ref[...], c_ref[...]], axis=1)
    x1_ref[...] = x_ref[...] + jnp.dot(mix, wo_ref[...], preferred_element_type=F32)
    h2 = _rms(x1_ref[...], g2_ref[...]).astype(BF16)
    for f0 in range(0, D_FF, FF_CHUNK):
        f1 = min(f0 + FF_CHUNK, D_FF)
        gate = jnp.dot(h2, wg_ref[:, f0:f1], preferred_element_type=F32)
        up = jnp.dot(h2, wu_ref[:, f0:f1], preferred_element_type=F32)
        act_ref[:, f0:f1] = (gate * (1.0 / (1.0 + jnp.exp(-gate))) * up).astype(BF16)
    x2 = x1_ref[...] + jnp.dot(act_ref[...], wd_ref[...], preferred_element_type=F32)
    o_ref[...] = _rms(x2, gf_ref[...]) if last else x2


def _ffn(x2d, a, b, c, wo, g2, wg, wu, wd, gf, layer, last):
    t = x2d.shape[0]
    rows = lambda w: pl.BlockSpec((TM_FFN, w), lambda i: (i, 0))
    return pl.pallas_call(
        functools.partial(_ffn_kernel, last=last),
        grid=(t // TM_FFN,),
        in_specs=[
            rows(D_MODEL), rows(POOL_WIDTH), rows(DIFF_WIDTH), rows(SB_WIDTH),
            _layer_weight((D_MODEL, D_MODEL), layer), _resident((1, D_MODEL)),
            _layer_weight((D_MODEL, D_FF), layer), _layer_weight((D_MODEL, D_FF), layer),
            _layer_weight((D_FF, D_MODEL), layer),
            _resident((1, D_MODEL)),
        ],
        out_specs=rows(D_MODEL),
        out_shape=jax.ShapeDtypeStruct((t, D_MODEL), F32),
        scratch_shapes=[
            pltpu.VMEM((TM_FFN, D_MODEL), F32),
            pltpu.VMEM((TM_FFN, D_FF), BF16),
        ],
        compiler_params=pltpu.CompilerParams(
            dimension_semantics=("arbitrary",), vmem_limit_bytes=VMEM_LIMIT),
        name="ffn_last" if last else "ffn",
    )(x2d, a, b, c, wo, g2.reshape(1, D_MODEL), wg, wu, wd, gf.reshape(1, D_MODEL))


def kernel(x, norm1_g, w_in, pool_w, pool_scale, lam_q1, lam_k1, lam_q2, lam_k2,
           diff_norm_g, sb_norm_g, w_out, norm2_g, w_gate, w_up, w_down, final_norm_g):
    batch, seq, d = x.shape
    depth = w_in.shape[0]
    assert d == D_MODEL and seq % TQ == 0 and seq % TP_POOL == 0
    assert (batch * seq) % TM_PROJ == 0 and (batch * seq) % TM_FFN == 0

    w_in_b, w_out_b = w_in.astype(BF16), w_out.astype(BF16)
    w_gate_b, w_up_b, w_down_b = w_gate.astype(BF16), w_up.astype(BF16), w_down.astype(BF16)
    x2d = x.reshape(batch * seq, d)
    for l in range(depth):
        u, p3, vt = _proj(x2d, norm1_g[l], w_in_b, l)
        w_bd = jax.scipy.linalg.block_diag(*[pool_w[l, gi] for gi in range(len(POOL_WINDOWS))])
        a = _pool(u, w_bd.astype(BF16), pool_scale[l], batch, seq)
        lam_init = 0.8 - 0.6 * math.exp(-0.3 * l)
        lamp = jnp.stack([lam_q1[l], lam_k1[l], lam_q2[l], lam_k2[l]])
        bo = _diff_attention(p3, vt, lamp, diff_norm_g[l], lam_init, batch, seq)
        co = _sb_attention(p3, sb_norm_g[l], batch, seq)
        x2d = _ffn(x2d, a, bo, co, w_out_b, norm2_g[l], w_gate_b, w_up_b, w_down_b,
                   final_norm_g, l, last=(l == depth - 1))
    return x2d.reshape(batch, seq, d)
```

```python
import functools
import math

import jax
import jax.numpy as jnp
from jax import lax
from jax.experimental import pallas as pl
from jax.experimental.pallas import tpu as pltpu

F32 = jnp.float32
BF16 = jnp.bfloat16

D_MODEL = 1024
POOL_WINDOWS = (2, 4, 8, 16)
POOL_WIDTH = 256
POOL_GROUP = 64
DIFF_WIDTH = 512
DIFF_HEAD_DIM = 64
DIFF_HEADS = 4
SB_WIDTH = 256
SB_HEAD_DIM = 64
SB_HEADS = 4
D_IN = POOL_WIDTH + 3 * DIFF_WIDTH + 3 * SB_WIDTH
D_FF = 2816
EPS = 1e-6

LANES = 128
N_SLABS = (D_IN - POOL_WIDTH) // LANES
SB_SLABS = SB_WIDTH // LANES
BLK_DQ, BLK_DK = 0, 1
SLAB_DV = 2 * DIFF_HEADS
BLK_SQ, BLK_SK, BLK_SV = 6, 7, 8

TM_PROJ = 512
TM_FFN = 512
TP_POOL = 512
POOL_HALO = 16
TQ = 256
TK = 256
FF_CHUNK = 512
VMEM_LIMIT = 56 * 1024 * 1024
DIFF_Q_SCALE = DIFF_HEAD_DIM ** -0.5 * math.log2(math.e)
SB_DEAD_MASS = 105.0

_NT = (((1,), (1,)), ((), ()))


def _resident(shape):
    return pl.BlockSpec(shape, lambda *_: (0,) * len(shape), pipeline_mode=pl.Buffered(1))


def _layer_weight(shape, layer):
    return pl.BlockSpec((None,) + shape, lambda *_: (layer, 0, 0), pipeline_mode=pl.Buffered(1))


def _rms(x, g):
    ms = jnp.mean(x * x, axis=-1, keepdims=True)
    return x * lax.rsqrt(ms + EPS) * g


def _proj_kernel(x_ref, g_ref, w_ref, u_ref, p_ref, vt_ref):
    h = _rms(x_ref[...], g_ref[...]).astype(BF16)
    u_ref[...] = jnp.dot(h, w_ref[:, :POOL_WIDTH], preferred_element_type=F32)
    for c0 in range(0, N_SLABS, 4):
        lo = POOL_WIDTH + c0 * LANES
        n = min(4, N_SLABS - c0)
        acc = jnp.dot(h, w_ref[:, lo:lo + n * LANES], preferred_element_type=F32)
        for j in range(n):
            slab = acc[:, j * LANES:(j + 1) * LANES]
            if c0 + j < DIFF_HEADS:
                slab = slab * DIFF_Q_SCALE
            p_ref[c0 + j] = slab.astype(BF16)
            head = c0 + j - SLAB_DV
            if 0 <= head < DIFF_HEADS:
                for c in range(TM_PROJ // TK):
                    vt_ref[head, c] = slab[c * TK:(c + 1) * TK, :].T.astype(BF16)


def _proj(x2d, g, w_stack, layer):
    t = x2d.shape[0]
    return pl.pallas_call(
        _proj_kernel,
        grid=(t // TM_PROJ,),
        in_specs=[
            pl.BlockSpec((TM_PROJ, D_MODEL), lambda i: (i, 0)),
            _resident((1, D_MODEL)),
            _layer_weight((D_MODEL, D_IN), layer),
        ],
        out_specs=[
            pl.BlockSpec((TM_PROJ, POOL_WIDTH), lambda i: (i, 0)),
            pl.BlockSpec((N_SLABS, TM_PROJ, LANES), lambda i: (0, i, 0)),
            pl.BlockSpec((DIFF_HEADS, TM_PROJ // TK, LANES, TK), lambda i: (0, i, 0, 0)),
        ],
        out_shape=[
            jax.ShapeDtypeStruct((t, POOL_WIDTH), F32),
            jax.ShapeDtypeStruct((N_SLABS, t, LANES), BF16),
            jax.ShapeDtypeStruct((DIFF_HEADS, t // TK, LANES, TK), BF16),
        ],
        compiler_params=pltpu.CompilerParams(
            dimension_semantics=("arbitrary",), vmem_limit_bytes=VMEM_LIMIT),
        name="proj",
    )(x2d, g.reshape(1, D_MODEL), w_stack)


def _pool_kernel(u_ref, halo_ref, w_ref, scale_ref, o_ref):
    i = pl.program_id(1)
    halo = jnp.where(i > 0, halo_ref[...], 0.0)
    e0 = jnp.concatenate([halo, u_ref[...]], axis=0)
    s2 = e0 + pltpu.roll(e0, 1, 0)
    s4 = s2 + pltpu.roll(s2, 2, 0)
    s8 = s4 + pltpu.roll(s4, 4, 0)
    s16 = s8 + pltpu.roll(s8, 8, 0)
    shape = e0.shape
    lane = lax.broadcasted_iota(jnp.int32, shape, 1)
    row = lax.broadcasted_iota(jnp.int32, shape, 0)
    g0, g1, g2 = lane < POOL_GROUP, lane < 2 * POOL_GROUP, lane < 3 * POOL_GROUP
    win = jnp.where(g0, s2, jnp.where(g1, s4, jnp.where(g2, s8, s16)))
    width = jnp.where(g0, POOL_WINDOWS[0],
                      jnp.where(g1, POOL_WINDOWS[1],
                                jnp.where(g2, POOL_WINDOWS[2], POOL_WINDOWS[3])))
    pos = i * TP_POOL + row - POOL_HALO
    cnt = jnp.maximum(jnp.minimum(pos + 1, width), 1).astype(F32)
    pooled = (win / cnt - e0)[POOL_HALO:]
    mixed = jnp.dot(pooled.astype(BF16), w_ref[...], preferred_element_type=F32)
    o_ref[...] = (mixed * scale_ref[...]).astype(BF16)


def _pool(u, w_blockdiag, scale, batch, seq):
    t = u.shape[0]
    nblk = seq // TP_POOL
    halo_per_blk = TP_POOL // POOL_HALO
    return pl.pallas_call(
        _pool_kernel,
        grid=(batch, nblk),
        in_specs=[
            pl.BlockSpec((TP_POOL, POOL_WIDTH), lambda b, i: (b * nblk + i, 0)),
            pl.BlockSpec((POOL_HALO, POOL_WIDTH),
                         lambda b, i: (jnp.maximum((b * nblk + i) * halo_per_blk - 1, 0), 0)),
            _resident((POOL_WIDTH, POOL_WIDTH)),
            _resident((1, POOL_WIDTH)),
        ],
        out_specs=pl.BlockSpec((TP_POOL, POOL_WIDTH), lambda b, i: (b * nblk + i, 0)),
        out_shape=jax.ShapeDtypeStruct((t, POOL_WIDTH), BF16),
        compiler_params=pltpu.CompilerParams(
            dimension_semantics=("arbitrary", "arbitrary"), vmem_limit_bytes=VMEM_LIMIT),
        name="pool",
    )(u, u, w_blockdiag, scale.reshape(1, POOL_WIDTH))


def _twice(a):
    return jnp.concatenate([a] * (TK // LANES), axis=1)


def _diff_kernel(q_ref, k_ref, vt_ref, lamp_ref, g_ref, o_ref,
                 m_ref, l_ref, acc_ref, s0_ref, s1_ref, *, lam_init):
    qi = pl.program_id(1)
    lane = lax.broadcasted_iota(jnp.int32, (TQ, LANES), 1)
    first_map = lane < DIFF_HEAD_DIM
    qms = []
    for h in range(DIFF_HEADS):
        q = q_ref[h]
        zero = jnp.zeros_like(q)
        qms += [jnp.where(first_map, q, zero), jnp.where(first_map, zero, q)]
    key = lax.broadcasted_iota(jnp.int32, (TK, TQ), 0)
    qry = lax.broadcasted_iota(jnp.int32, (TK, TQ), 1)
    causal = key <= qry

    s_refs = (s0_ref, s1_ref)

    def score_tile(kb, slot):
        start = pl.multiple_of(kb * TK, TK)
        for i, qm in enumerate(qms):
            s_refs[slot][i] = lax.dot_general(k_ref[i // 2, pl.ds(start, TK), :], qm, _NT,
                                              preferred_element_type=F32)

    def softmax_tile(kb, slot, diagonal):
        probs, alphas = [], []
        for i in range(len(qms)):
            s = s_refs[slot][i]
            if diagonal:
                s = jnp.where(causal, s, -jnp.inf)
            m_prev = m_ref[i]
            m_new = jnp.maximum(m_prev, jnp.max(s, axis=0, keepdims=True))
            alphas.append(jnp.exp2(m_prev - m_new))
            p = jnp.exp2(s - m_new)
            l_ref[i] = alphas[i] * l_ref[i] + jnp.sum(p, axis=0, keepdims=True)
            m_ref[i] = m_new
            probs.append(p.astype(BF16))
        for i, p in enumerate(probs):
            pv = jnp.dot(vt_ref[i // 2, kb], p, preferred_element_type=F32)
            acc_ref[i] = alphas[i] * acc_ref[i] + pv

    m_ref[...] = jnp.full(m_ref.shape, -jnp.inf, F32)
    l_ref[...] = jnp.zeros(l_ref.shape, F32)
    acc_ref[...] = jnp.zeros(acc_ref.shape, F32)
    score_tile(qi, 1)
    score_tile(0, 0)
    softmax_tile(qi, 1, True)

    def pair(jj, carry):
        kb = 2 * jj
        score_tile(kb + 1, 1)
        softmax_tile(kb, 0, False)
        score_tile(jnp.minimum(kb + 2, qi), 0)
        softmax_tile(kb + 1, 1, False)
        return carry

    lax.fori_loop(0, qi // 2, pair, 0)
    pl.when(qi % 2 == 1)(functools.partial(softmax_tile, qi - 1, 0, False))

    lamp = lamp_ref[...]
    lam = (jnp.exp(jnp.sum(lamp[0:1] * lamp[1:2], axis=1, keepdims=True))
           - jnp.exp(jnp.sum(lamp[2:3] * lamp[3:4], axis=1, keepdims=True)) + lam_init)
    for h in range(DIFF_HEADS):
        o = acc_ref[2 * h] / l_ref[2 * h] - lam * (acc_ref[2 * h + 1] / l_ref[2 * h + 1])
        ms = jnp.mean(o * o, axis=0, keepdims=True)
        y = o * lax.rsqrt(ms + EPS) * g_ref[...] * (1.0 - lam_init)
        o_ref[:, h * LANES:(h + 1) * LANES] = y.T.astype(BF16)


def _diff_attention(p3, vt, lamp, g, lam_init, batch, seq):
    t = p3.shape[1]
    nq = seq // TQ
    nk = seq // TK
    vdim = 2 * DIFF_HEAD_DIM
    return pl.pallas_call(
        functools.partial(_diff_kernel, lam_init=lam_init),
        grid=(batch, nq),
        in_specs=[
            pl.BlockSpec((DIFF_HEADS, TQ, LANES), lambda b, i: (BLK_DQ, b * nq + i, 0)),
            pl.BlockSpec((DIFF_HEADS, seq, LANES), lambda b, i: (BLK_DK, b, 0)),
            pl.BlockSpec((DIFF_HEADS, nk, vdim, TK), lambda b, i: (0, b, 0, 0)),
            _resident((4, DIFF_HEAD_DIM)),
            _resident((vdim, 1)),
        ],
        out_specs=pl.BlockSpec((TQ, DIFF_WIDTH), lambda b, i: (b * nq + i, 0)),
        out_shape=jax.ShapeDtypeStruct((t, DIFF_WIDTH), BF16),
        scratch_shapes=[
            pltpu.VMEM((2 * DIFF_HEADS, 1, TQ), F32),
            pltpu.VMEM((2 * DIFF_HEADS, 1, TQ), F32),
            pltpu.VMEM((2 * DIFF_HEADS, vdim, TQ), F32),
            pltpu.VMEM((2 * DIFF_HEADS, TK, TQ), F32),
            pltpu.VMEM((2 * DIFF_HEADS, TK, TQ), F32),
        ],
        compiler_params=pltpu.CompilerParams(
            dimension_semantics=("arbitrary",) * 2, vmem_limit_bytes=VMEM_LIMIT),
        name="diff_attn",
    )(p3, p3, vt, lamp, g.reshape(vdim, 1))


def _sb_kernel(q_ref, k_ref, v_ref, g_ref, o_ref, tri_ref, carry_ref, acc_ref, z0_ref, z1_ref):
    b, qi = pl.program_id(0), pl.program_id(1)

    @pl.when((b == 0) & (qi == 0))
    def _():
        j = lax.broadcasted_iota(jnp.int32, (2 * TK, TK), 0)
        s = lax.broadcasted_iota(jnp.int32, (2 * TK, TK), 1)
        j = jnp.where(j >= TK, j - TK, j)
        tri_ref[...] = jnp.where(j >= s, 1.0, 0.0).astype(BF16)

    scale = jnp.asarray(SB_HEAD_DIM ** -0.5, BF16)
    lane = lax.broadcasted_iota(jnp.int32, (TQ, LANES), 1)
    lower = lane < SB_HEAD_DIM
    qms = []
    for sl in range(SB_SLABS):
        q = q_ref[sl] * scale
        zero = jnp.zeros_like(q)
        qms += [jnp.where(lower, q, zero), jnp.where(lower, zero, q)]
    row = lax.broadcasted_iota(jnp.int32, (TQ, TK), 0)
    col = lax.broadcasted_iota(jnp.int32, (TQ, TK), 1)
    strict = col < row

    z_refs = (z0_ref, z1_ref)

    def score_tile(kb, slot):
        start = pl.multiple_of(kb * TK, TK)
        for hd, qm in enumerate(qms):
            z_refs[slot][hd] = lax.dot_general(qm, k_ref[hd // 2, pl.ds(start, TK), :], _NT,
                                               preferred_element_type=F32)

    def weight_tile(kb, slot, diagonal):
        start = pl.multiple_of(kb * TK, TK)
        zs = [z_refs[slot][hd] for hd in range(SB_HEADS)]
        splits = []
        for z in zs:
            sp = jnp.maximum(z, 0.0) + jnp.log(1.0 + jnp.exp(-jnp.abs(z)))
            if diagonal:
                sp = jnp.where(strict, sp, 0.0)
            hi = sp.astype(BF16)
            lo = (sp - hi.astype(F32)).astype(BF16)
            splits.append(jnp.concatenate([hi, lo], axis=1))
        sums = [jnp.dot(s, tri_ref[...], preferred_element_type=F32) for s in splits]
        weights = []
        for hd, (z, c) in enumerate(zip(zs, sums)):
            if not diagonal:
                c = c + _twice(carry_ref[hd])
            a = jnp.exp(z - c)
            if diagonal:
                a = jnp.where(strict, a, 0.0)
            carry_ref[hd] = jnp.broadcast_to(c[:, 0:1], (TQ, LANES))
            weights.append(a.astype(BF16))
        for hd, a in enumerate(weights):
            pv = jnp.dot(a, v_ref[hd // 2, pl.ds(start, TK), :], preferred_element_type=F32)
            acc_ref[hd] = pv if diagonal else acc_ref[hd] + pv

    score_tile(qi, 0)
    score_tile(jnp.maximum(qi - 1, 0), 1)
    weight_tile(qi, 0, True)

    def step(j, slot):
        kb = qi - 1 - j
        score_tile(jnp.maximum(kb - 1, 0), 1 - slot)
        weight_tile(kb, slot, False)

    def open_rows():
        return (jnp.min(carry_ref[...]) < SB_DEAD_MASS).astype(jnp.int32)

    def body(state):
        j, _ = state
        for slot in range(2):
            pl.when((j + 1) % 2 == slot)(functools.partial(step, j, slot))
        return j + 1, open_rows()

    lax.while_loop(lambda state: (state[0] < qi) & (state[1] > 0), body,
                   (jnp.int32(0), open_rows()))

    lane_o = lax.broadcasted_iota(jnp.int32, (TQ, LANES), 1)
    first = lane_o < SB_HEAD_DIM
    for sl in range(SB_SLABS):
        o = jnp.where(first, acc_ref[2 * sl], acc_ref[2 * sl + 1])
        sq = o * o
        ms = jnp.where(first,
                       jnp.sum(jnp.where(first, sq, 0.0), axis=1, keepdims=True),
                       jnp.sum(jnp.where(first, 0.0, sq), axis=1, keepdims=True)) / SB_HEAD_DIM
        o_ref[:, sl * LANES:(sl + 1) * LANES] = (
            o * lax.rsqrt(ms + EPS) * g_ref[...]).astype(BF16)


def _sb_attention(p3, g, batch, seq):
    t = p3.shape[1]
    nq = seq // TQ
    return pl.pallas_call(
        _sb_kernel,
        grid=(batch, nq),
        in_specs=[
            pl.BlockSpec((SB_SLABS, TQ, LANES), lambda b, i: (BLK_SQ, b * nq + i, 0)),
            pl.BlockSpec((SB_SLABS, seq, LANES), lambda b, i: (BLK_SK, b, 0)),
            pl.BlockSpec((SB_SLABS, seq, LANES), lambda b, i: (BLK_SV, b, 0)),
            _resident((1, LANES)),
        ],
        out_specs=pl.BlockSpec((TQ, SB_WIDTH), lambda b, i: (b * nq + i, 0)),
        out_shape=jax.ShapeDtypeStruct((t, SB_WIDTH), BF16),
        scratch_shapes=[
            pltpu.VMEM((2 * TK, TK), BF16),
            pltpu.VMEM((SB_HEADS, TQ, LANES), F32),
            pltpu.VMEM((SB_HEADS, TQ, LANES), F32),
            pltpu.VMEM((SB_HEADS, TQ, TK), F32),
            pltpu.VMEM((SB_HEADS, TQ, TK), F32),
        ],
        compiler_params=pltpu.CompilerParams(
            dimension_semantics=("arbitrary",) * 2, vmem_limit_bytes=VMEM_LIMIT),
        name="sb_attn",
    )(p3, p3, p3, jnp.concatenate([g, g]).reshape(1, LANES))


def _ffn_kernel(x_ref, a_ref, b_ref, c_ref, wo_ref, g2_ref, wg_ref, wu_ref, wd_ref, gf_ref,
                o_ref, x1_ref, act_ref, *, last):
    mix = jnp.concatenate([a_ref[...], b_ref[...], c_ref[...]], axis=1)
    x1_ref[...] = x_ref[...] + jnp.dot(mix, wo_ref[...], preferred_element_type=F32)
    h2 = _rms(x1_ref[...], g2_ref[...]).astype(BF16)
    for f0 in range(0, D_FF, FF_CHUNK):
        f1 = min(f0 + FF_CHUNK, D_FF)
        gate = jnp.dot(h2, wg_ref[:, f0:f1], preferred_element_type=F32)
        up = jnp.dot(h2, wu_ref[:, f0:f1], preferred_element_type=F32)
        act_ref[:, f0:f1] = (gate * (1.0 / (1.0 + jnp.exp(-gate))) * up).astype(BF16)
    x2 = x1_ref[...] + jnp.dot(act_ref[...], wd_ref[...], preferred_element_type=F32)
    o_ref[...] = _rms(x2, gf_ref[...]) if last else x2


def _ffn(x2d, a, b, c, wo, g2, wg, wu, wd, gf, layer, last):
    t = x2d.shape[0]
    rows = lambda w: pl.BlockSpec((TM_FFN, w), lambda i: (i, 0))
    return pl.pallas_call(
        functools.partial(_ffn_kernel, last=last),
        grid=(t // TM_FFN,),
        in_specs=[
            rows(D_MODEL), rows(POOL_WIDTH), rows(DIFF_WIDTH), rows(SB_WIDTH),
            _layer_weight((D_MODEL, D_MODEL), layer), _resident((1, D_MODEL)),
            _layer_weight((D_MODEL, D_FF), layer), _layer_weight((D_MODEL, D_FF), layer),
            _layer_weight((D_FF, D_MODEL), layer),
            _resident((1, D_MODEL)),
        ],
        out_specs=rows(D_MODEL),
        out_shape=jax.ShapeDtypeStruct((t, D_MODEL), F32),
        scratch_shapes=[
            pltpu.VMEM((TM_FFN, D_MODEL), F32),
            pltpu.VMEM((TM_FFN, D_FF), BF16),
        ],
        compiler_params=pltpu.CompilerParams(
            dimension_semantics=("arbitrary",), vmem_limit_bytes=VMEM_LIMIT),
        name="ffn_last" if last else "ffn",
    )(x2d, a, b, c, wo, g2.reshape(1, D_MODEL), wg, wu, wd, gf.reshape(1, D_MODEL))


def kernel(x, norm1_g, w_in, pool_w, pool_scale, lam_q1, lam_k1, lam_q2, lam_k2,
           diff_norm_g, sb_norm_g, w_out, norm2_g, w_gate, w_up, w_down, final_norm_g):
    batch, seq, d = x.shape
    depth = w_in.shape[0]
    assert d == D_MODEL and seq % TQ == 0 and seq % TP_POOL == 0
    assert (batch * seq) % TM_PROJ == 0 and (batch * seq) % TM_FFN == 0

    w_in_b, w_out_b = w_in.astype(BF16), w_out.astype(BF16)
    w_gate_b, w_up_b, w_down_b = w_gate.astype(BF16), w_up.astype(BF16), w_down.astype(BF16)
    x2d = x.reshape(batch * seq, d)
    for l in range(depth):
        u, p3, vt = _proj(x2d, norm1_g[l], w_in_b, l)
        w_bd = jax.scipy.linalg.block_diag(*[pool_w[l, gi] for gi in range(len(POOL_WINDOWS))])
        a = _pool(u, w_bd.astype(BF16), pool_scale[l], batch, seq)
        lam_init = 0.8 - 0.6 * math.exp(-0.3 * l)
        lamp = jnp.stack([lam_q1[l], lam_k1[l], lam_q2[l], lam_k2[l]])
        bo = _diff_attention(p3, vt, lamp, diff_norm_g[l], lam_init, batch, seq)
        co = _sb_attention(p3, sb_norm_g[l], batch, seq)
        x2d = _ffn(x2d, a, bo, co, w_out_b, norm2_g[l], w_gate_b, w_up_b, w_down_b,
                   final_norm_g, l, last=(l == depth - 1))
    return x2d.reshape(batch, seq, d)
```

```python
import functools
import math

import jax
import jax.numpy as jnp
from jax import lax
from jax.experimental import pallas as pl
from jax.experimental.pallas import tpu as pltpu

F32 = jnp.float32
BF16 = jnp.bfloat16

D_MODEL = 1024
POOL_WINDOWS = (2, 4, 8, 16)
POOL_WIDTH = 256
POOL_GROUP = 64
DIFF_WIDTH = 512
DIFF_HEAD_DIM = 64
DIFF_HEADS = 4
SB_WIDTH = 256
SB_HEAD_DIM = 64
SB_HEADS = 4
D_IN = POOL_WIDTH + 3 * DIFF_WIDTH + 3 * SB_WIDTH
D_FF = 2816
EPS = 1e-6

LANES = 128
N_SLABS = (D_IN - POOL_WIDTH) // LANES
SB_SLABS = SB_WIDTH // LANES
BLK_DQ, BLK_DK = 0, 1
SLAB_DV = 2 * DIFF_HEADS
BLK_SQ, BLK_SK, BLK_SV = 6, 7, 8

TM_PROJ = 512
TM_FFN = 512
POOL_HALO = 16
TQ = 256
TK = 256
FF_CHUNK = 512
VMEM_LIMIT = 56 * 1024 * 1024
DIFF_VDIM = 2 * DIFF_HEAD_DIM
VT_ROWS = DIFF_VDIM + 16
DIFF_Q_SCALE =DIFF_HEAD_DIM ** -0.5 * math.log2(math.e)
SB_DEAD_MASS = 105.0

_NT = (((1,), (1,)), ((), ()))


def _resident(shape):
    return pl.BlockSpec(shape, lambda *_: (0,) * len(shape), pipeline_mode=pl.Buffered(1))


def _layer_weight(shape, layer):
    return pl.BlockSpec((None,) + shape, lambda *_: (layer, 0, 0), pipeline_mode=pl.Buffered(1))


def _rms(x, g):
    ms = jnp.mean(x * x, axis=-1, keepdims=True)
    return x * lax.rsqrt(ms + EPS) * g


def _pool_mix(e0, first_pos, w_blockdiag, scale):
    s2 = e0 + pltpu.roll(e0, 1, 0)
    s4 = s2 + pltpu.roll(s2, 2, 0)
    s8 = s4 + pltpu.roll(s4, 4, 0)
    s16 = s8 + pltpu.roll(s8, 8, 0)
    shape = e0.shape
    lane = lax.broadcasted_iota(jnp.int32, shape, 1)
    row = lax.broadcasted_iota(jnp.int32, shape, 0)
    g0, g1, g2 = lane < POOL_GROUP, lane < 2 * POOL_GROUP, lane < 3 * POOL_GROUP
    win = jnp.where(g0, s2, jnp.where(g1, s4, jnp.where(g2, s8, s16)))
    width = jnp.where(g0, POOL_WINDOWS[0],
                      jnp.where(g1, POOL_WINDOWS[1],
                                jnp.where(g2, POOL_WINDOWS[2], POOL_WINDOWS[3])))
    pos = first_pos + row - POOL_HALO
    cnt = jnp.maximum(jnp.minimum(pos + 1, width), 1).astype(F32)
    pooled = (win / cnt - e0)[POOL_HALO:]
    mixed = jnp.dot(pooled.astype(BF16), w_blockdiag, preferred_element_type=F32)
    return (mixed * scale).astype(BF16)


def _proj_kernel(x_ref, g_ref, w_ref, wp_ref, sp_ref, a_ref, p_ref, vt_ref, halo_ref,
                 *, steps_per_seq):
    i = pl.program_id(0) % steps_per_seq

    @pl.when(i == 0)
    def _():
        halo_ref[...] = jnp.zeros(halo_ref.shape, F32)

    h = _rms(x_ref[...], g_ref[...]).astype(BF16)
    u = jnp.dot(h, w_ref[:, :POOL_WIDTH], preferred_element_type=F32)
    for c0 in range(0, N_SLABS, 4):
        lo = POOL_WIDTH + c0 * LANES
        n = min(4, N_SLABS - c0)
        acc = jnp.dot(h, w_ref[:, lo:lo + n * LANES], preferred_element_type=F32)
        for j in range(n):
            slab = acc[:, j * LANES:(j + 1) * LANES]
            if c0 + j < DIFF_HEADS:
                slab = slab * DIFF_Q_SCALE
            p_ref[c0 + j] = slab.astype(BF16)
            head = c0 + j - SLAB_DV
            if 0 <= head < DIFF_HEADS:
                for c in range(TM_PROJ // TK):
                    vt_ref[head, c, :DIFF_VDIM, :] = slab[c * TK:(c + 1) * TK, :].T.astype(BF16)
                    vt_ref[head, c, DIFF_VDIM:, :] = jnp.ones((VT_ROWS - DIFF_VDIM, TK), BF16)
    e0 = jnp.concatenate([halo_ref[...], u], axis=0)
    a_ref[...] = _pool_mix(e0, i * TM_PROJ, wp_ref[...], sp_ref[...])
    halo_ref[...] = u[TM_PROJ - POOL_HALO:]


def _proj(x2d, g, w_stack, layer, w_blockdiag, pool_scale, seq):
    t = x2d.shape[0]
    return pl.pallas_call(
        functools.partial(_proj_kernel, steps_per_seq=seq // TM_PROJ),
        grid=(t // TM_PROJ,),
        in_specs=[
            pl.BlockSpec((TM_PROJ, D_MODEL), lambda i: (i, 0)),
            _resident((1, D_MODEL)),
            _layer_weight((D_MODEL, D_IN), layer),
            _resident((POOL_WIDTH, POOL_WIDTH)),
            _resident((1, POOL_WIDTH)),
        ],
        out_specs=[
            pl.BlockSpec((TM_PROJ, POOL_WIDTH), lambda i: (i, 0)),
            pl.BlockSpec((N_SLABS, TM_PROJ, LANES), lambda i: (0, i, 0)),
            pl.BlockSpec((DIFF_HEADS, TM_PROJ // TK, VT_ROWS, TK), lambda i: (0, i, 0, 0)),
        ],
        out_shape=[
            jax.ShapeDtypeStruct((t, POOL_WIDTH), BF16),
            jax.ShapeDtypeStruct((N_SLABS, t, LANES), BF16),
            jax.ShapeDtypeStruct((DIFF_HEADS, t // TK, VT_ROWS, TK), BF16),
        ],
        scratch_shapes=[pltpu.VMEM((POOL_HALO, POOL_WIDTH), F32)],
        compiler_params=pltpu.CompilerParams(
            dimension_semantics=("arbitrary",), vmem_limit_bytes=VMEM_LIMIT),
        name="proj",
    )(x2d, g.reshape(1, D_MODEL), w_stack, w_blockdiag, pool_scale.reshape(1, POOL_WIDTH))


def _twice(a):
    return jnp.concatenate([a] * (TK // LANES), axis=1)


def _diff_kernel(q_ref, k_ref, vt_ref, lamp_ref, g_ref, o_ref,
                 m_ref, acc_ref, s0_ref, s1_ref, *, lam_init):
    qi = pl.program_id(1)
    lane = lax.broadcasted_iota(jnp.int32, (TQ, LANES), 1)
    first_map = lane < DIFF_HEAD_DIM
    qms = []
    for h in range(DIFF_HEADS):
        q = q_ref[h]
        zero = jnp.zeros_like(q)
        qms += [jnp.where(first_map, q, zero), jnp.where(first_map, zero, q)]
    key = lax.broadcasted_iota(jnp.int32, (TK, TQ), 0)
    qry = lax.broadcasted_iota(jnp.int32, (TK, TQ), 1)
    causal = key <= qry

    s_refs = (s0_ref, s1_ref)

    def score_tile(kb, slot):
        start = pl.multiple_of(kb * TK, TK)
        for i, qm in enumerate(qms):
            s_refs[slot][i] = lax.dot_general(k_ref[i // 2, pl.ds(start, TK), :], qm, _NT,
                                              preferred_element_type=F32)

    def softmax_tile(kb, slot, diagonal):
        probs, alphas = [], []
        for i in range(len(qms)):
            s = s_refs[slot][i]
            if diagonal:
                s = jnp.where(causal, s, -jnp.inf)
            m_prev = m_ref[i]
            m_new = jnp.maximum(m_prev, jnp.max(s, axis=0, keepdims=True))
            alphas.append(jnp.exp2(m_prev - m_new))
            m_ref[i] = m_new
            probs.append(jnp.exp2(s - m_new).astype(BF16))
        for i, p in enumerate(probs):
            pv = jnp.dot(vt_ref[i // 2, kb], p, preferred_element_type=F32)
            acc_ref[i] = alphas[i] * acc_ref[i] + pv

    m_ref[...] = jnp.full(m_ref.shape, -jnp.inf, F32)
    acc_ref[...] = jnp.zeros(acc_ref.shape, F32)
    score_tile(qi, 1)
    score_tile(0, 0)
    softmax_tile(qi, 1, True)

    def pair(jj, carry):
        kb = 2 * jj
        score_tile(kb + 1, 1)
        softmax_tile(kb, 0, False)
        score_tile(jnp.minimum(kb + 2, qi), 0)
        softmax_tile(kb + 1, 1, False)
        return carry

    lax.fori_loop(0, qi // 2, pair, 0)
    pl.when(qi % 2 == 1)(functools.partial(softmax_tile, qi - 1, 0, False))

    lamp = lamp_ref[...]
    lam = (jnp.exp(jnp.sum(lamp[0:1] * lamp[1:2], axis=1, keepdims=True))
           - jnp.exp(jnp.sum(lamp[2:3] * lamp[3:4], axis=1, keepdims=True)) + lam_init)
    for h in range(DIFF_HEADS):
        num1, den1 = acc_ref[2 * h, :DIFF_VDIM, :], acc_ref[2 * h, DIFF_VDIM:DIFF_VDIM + 1, :]
        num2, den2 = (acc_ref[2 * h + 1, :DIFF_VDIM, :],
                      acc_ref[2 * h + 1, DIFF_VDIM:DIFF_VDIM + 1, :])
        o = num1 / den1 - lam * (num2 / den2)
        ms = jnp.mean(o * o, axis=0, keepdims=True)
        y = o * lax.rsqrt(ms + EPS) * g_ref[...] * (1.0 - lam_init)
        o_ref[:, h * LANES:(h + 1) * LANES] = y.T.astype(BF16)


def _diff_attention(p3, vt, lamp, g, lam_init, batch, seq):
    t = p3.shape[1]
    nq = seq // TQ
    nk = seq // TK
    return pl.pallas_call(
        functools.partial(_diff_kernel, lam_init=lam_init),
        grid=(batch, nq),
        in_specs=[
            pl.BlockSpec((DIFF_HEADS, TQ, LANES), lambda b, i: (BLK_DQ, b * nq + i, 0)),
            pl.BlockSpec((DIFF_HEADS, seq, LANES), lambda b, i: (BLK_DK, b, 0)),
            pl.BlockSpec((DIFF_HEADS, nk, VT_ROWS, TK), lambda b, i: (0, b, 0, 0)),
            _resident((4, DIFF_HEAD_DIM)),
            _resident((DIFF_VDIM, 1)),
        ],
        out_specs=pl.BlockSpec((TQ, DIFF_WIDTH), lambda b, i: (b * nq + i, 0)),
        out_shape=jax.ShapeDtypeStruct((t, DIFF_WIDTH), BF16),
        scratch_shapes=[
            pltpu.VMEM((2 * DIFF_HEADS, 1, TQ), F32),
            pltpu.VMEM((2 * DIFF_HEADS, VT_ROWS, TQ), F32),
            pltpu.VMEM((2 * DIFF_HEADS, TK, TQ), F32),
            pltpu.VMEM((2 * DIFF_HEADS, TK, TQ), F32),
        ],
        compiler_params=pltpu.CompilerParams(
            dimension_semantics=("arbitrary",) * 2, vmem_limit_bytes=VMEM_LIMIT),
        name="diff_attn",
    )(p3, p3, vt, lamp, g.reshape(DIFF_VDIM, 1))


def _sb_kernel(q_ref, k_ref, v_ref, g_ref, o_ref, tri_ref, carry_ref, acc_ref, z0_ref, z1_ref):
    b, qi = pl.program_id(0), pl.program_id(1)

    @pl.when((b == 0) & (qi == 0))
    def _():
        j = lax.broadcasted_iota(jnp.int32, (2 * TK, TK), 0)
        s = lax.broadcasted_iota(jnp.int32, (2 * TK, TK), 1)
        j = jnp.where(j >= TK, j - TK, j)
        tri_ref[...] = jnp.where(j >= s, 1.0, 0.0).astype(BF16)

    scale = jnp.asarray(SB_HEAD_DIM ** -0.5, BF16)
    lane = lax.broadcasted_iota(jnp.int32, (TQ, LANES), 1)
    lower = lane < SB_HEAD_DIM
    qms = []
    for sl in range(SB_SLABS):
        q = q_ref[sl] * scale
        zero = jnp.zeros_like(q)
        qms += [jnp.where(lower, q, zero), jnp.where(lower, zero, q)]
    row = lax.broadcasted_iota(jnp.int32, (TQ, TK), 0)
    col = lax.broadcasted_iota(jnp.int32, (TQ, TK), 1)
    strict = col < row

    z_refs = (z0_ref, z1_ref)

    def score_tile(kb, slot):
        start = pl.multiple_of(kb * TK, TK)
        for hd, qm in enumerate(qms):
            z_refs[slot][hd] = lax.dot_general(qm, k_ref[hd // 2, pl.ds(start, TK), :], _NT,
                                               preferred_element_type=F32)

    def weight_tile(kb, slot, diagonal):
        start = pl.multiple_of(kb * TK, TK)
        zs = [z_refs[slot][hd] for hd in range(SB_HEADS)]
        splits = []
        for z in zs:
            sp = jnp.maximum(z, 0.0) + jnp.log(1.0 + jnp.exp(-jnp.abs(z)))
            if diagonal:
                sp = jnp.where(strict, sp, 0.0)
            hi = sp.astype(BF16)
            lo = (sp - hi.astype(F32)).astype(BF16)
            splits.append(jnp.concatenate([hi, lo], axis=1))
        sums = [jnp.dot(s, tri_ref[...], preferred_element_type=F32) for s in splits]
        weights = []
        for hd, (z, c) in enumerate(zip(zs, sums)):
            if not diagonal:
                c = c + _twice(carry_ref[hd])
            a = jnp.exp(z - c)
            if diagonal:
                a = jnp.where(strict, a, 0.0)
            carry_ref[hd] = jnp.broadcast_to(c[:, 0:1], (TQ, LANES))
            weights.append(a.astype(BF16))
        for hd, a in enumerate(weights):
            pv = jnp.dot(a, v_ref[hd // 2, pl.ds(start, TK), :], preferred_element_type=F32)
            acc_ref[hd] = pv if diagonal else acc_ref[hd] + pv

    score_tile(qi, 0)
    score_tile(jnp.maximum(qi - 1, 0), 1)
    weight_tile(qi, 0, True)

    def step(j, slot):
        kb = qi - 1 - j
        score_tile(jnp.maximum(kb - 1, 0), 1 - slot)
        weight_tile(kb, slot, False)

    def open_rows():
        return (jnp.min(carry_ref[...]) < SB_DEAD_MASS).astype(jnp.int32)

    def body(state):
        j, _ = state
        for slot in range(2):
            pl.when((j + 1) % 2 == slot)(functools.partial(step, j, slot))
        return j + 1, open_rows()

    lax.while_loop(lambda state: (state[0] < qi) & (state[1] > 0), body,
                   (jnp.int32(0), open_rows()))

    lane_o = lax.broadcasted_iota(jnp.int32, (TQ, LANES), 1)
    first = lane_o < SB_HEAD_DIM
    for sl in range(SB_SLABS):
        o = jnp.where(first, acc_ref[2 * sl], acc_ref[2 * sl + 1])
        sq = o * o
        ms = jnp.where(first,
                       jnp.sum(jnp.where(first, sq, 0.0), axis=1, keepdims=True),
                       jnp.sum(jnp.where(first, 0.0, sq), axis=1, keepdims=True)) / SB_HEAD_DIM
        o_ref[:, sl * LANES:(sl + 1) * LANES] = (
            o * lax.rsqrt(ms + EPS) * g_ref[...]).astype(BF16)


def _sb_attention(p3, g, batch, seq):
    t = p3.shape[1]
    nq = seq // TQ
    return pl.pallas_call(
        _sb_kernel,
        grid=(batch, nq),
        in_specs=[
            pl.BlockSpec((SB_SLABS, TQ, LANES), lambda b, i: (BLK_SQ, b * nq + i, 0)),
            pl.BlockSpec((SB_SLABS, seq, LANES), lambda b, i: (BLK_SK, b, 0)),
            pl.BlockSpec((SB_SLABS, seq, LANES), lambda b, i: (BLK_SV, b, 0)),
            _resident((1, LANES)),
        ],
        out_specs=pl.BlockSpec((TQ, SB_WIDTH), lambda b, i: (b * nq + i, 0)),
        out_shape=jax.ShapeDtypeStruct((t, SB_WIDTH), BF16),
        scratch_shapes=[
            pltpu.VMEM((2 * TK, TK), BF16),
            pltpu.VMEM((SB_HEADS, TQ, LANES), F32),
            pltpu.VMEM((SB_HEADS, TQ, LANES), F32),
            pltpu.VMEM((SB_HEADS, TQ, TK), F32),
            pltpu.VMEM((SB_HEADS, TQ, TK), F32),
        ],
        compiler_params=pltpu.CompilerParams(
            dimension_semantics=("arbitrary",) * 2, vmem_limit_bytes=VMEM_LIMIT),
        name="sb_attn",
    )(p3, p3, p3, jnp.concatenate([g, g]).reshape(1, LANES))


def _ffn_kernel(x_ref, a_ref, b_ref, c_ref, wo_ref, g2_ref, wg_ref, wu_ref, wd_ref, gf_ref,
                o_ref, x1_ref, act_ref, *, last):
    mix = jnp.concatenate([a_ref[...], b_ref[...], c_ref[...]], axis=1)
    x1_ref[...] = x_ref[...] + jnp.dot(mix, wo_ref[...], preferred_element_type=F32)
    h2 = _rms(x1_ref[...], g2_ref[...]).astype(BF16)
    for f0 in range(0, D_FF, FF_CHUNK):
        f1 = min(f0 + FF_CHUNK, D_FF)
        gate = jnp.dot(h2, wg_ref[:, f0:f1], preferred_element_type=F32)
        up = jnp.dot(h2, wu_ref[:, f0:f1], preferred_element_type=F32)
        act_ref[:, f0:f1] = (gate * (1.0 / (1.0 + jnp.exp(-gate))) * up).astype(BF16)
    x2 = x1_ref[...] + jnp.dot(act_ref[...], wd_ref[...], preferred_element_type=F32)
    o_ref[...] = _rms(x2, gf_ref[...]) if last else x2


def _ffn(x2d, a, b, c, wo, g2, wg, wu, wd, gf, layer, last):
    t = x2d.shape[0]
    rows = lambda w: pl.BlockSpec((TM_FFN, w), lambda i: (i, 0))
    return pl.pallas_call(
        functools.partial(_ffn_kernel, last=last),
        grid=(t // TM_FFN,),
        in_specs=[
            rows(D_MODEL), rows(POOL_WIDTH), rows(DIFF_WIDTH), rows(SB_WIDTH),
            _layer_weight((D_MODEL, D_MODEL), layer), _resident((1, D_MODEL)),
            _layer_weight((D_MODEL, D_FF), layer), _layer_weight((D_MODEL, D_FF), layer),
            _layer_weight((D_FF, D_MODEL), layer),
            _resident((1, D_MODEL)),
        ],
        out_specs=rows(D_MODEL),
        out_shape=jax.ShapeDtypeStruct((t, D_MODEL), F32),
        scratch_shapes=[
            pltpu.VMEM((TM_FFN, D_MODEL), F32),
            pltpu.VMEM((TM_FFN, D_FF), BF16),
        ],
        compiler_params=pltpu.CompilerParams(
            dimension_semantics=("arbitrary",), vmem_limit_bytes=VMEM_LIMIT),
        name="ffn_last" if last else "ffn",
    )(x2d, a, b, c, wo, g2.reshape(1, D_MODEL), wg, wu, wd, gf.reshape(1, D_MODEL))


def kernel(x, norm1_g, w_in, pool_w, pool_scale, lam_q1, lam_k1, lam_q2, lam_k2,
           diff_norm_g, sb_norm_g, w_out, norm2_g, w_gate, w_up, w_down, final_norm_g):
    batch, seq, d = x.shape
    depth = w_in.shape[0]
    assert d == D_MODEL and seq % TQ == 0 and seq % TM_PROJ == 0
    assert (batch * seq) % TM_FFN == 0

    w_in_b, w_out_b = w_in.astype(BF16), w_out.astype(BF16)
    w_gate_b, w_up_b, w_down_b = w_gate.astype(BF16), w_up.astype(BF16), w_down.astype(BF16)
    x2d = x.reshape(batch * seq, d)
    for l in range(depth):
        w_bd = jax.scipy.linalg.block_diag(*[pool_w[l, gi] for gi in range(len(POOL_WINDOWS))])
        a, p3, vt = _proj(x2d, norm1_g[l], w_in_b, l, w_bd.astype(BF16), pool_scale[l], seq)
        lam_init = 0.8 - 0.6 * math.exp(-0.3 * l)
        lamp = jnp.stack([lam_q1[l], lam_k1[l], lam_q2[l], lam_k2[l]])
        bo = _diff_attention(p3, vt, lamp, diff_norm_g[l], lam_init, batch, seq)
        co = _sb_attention(p3, sb_norm_g[l], batch, seq)
        x2d = _ffn(x2d, a, bo, co, w_out_b, norm2_g[l], w_gate_b, w_up_b, w_down_b,
                   final_norm_g, l, last=(l == depth - 1))
    return x2d.reshape(batch, seq, d)
```

```python
import functools
import math

import jax
import jax.numpy as jnp
from jax import lax
from jax.experimental import pallas as pl
from jax.experimental.pallas import tpu as pltpu

F32 = jnp.float32
BF16 = jnp.bfloat16

D_MODEL = 1024
POOL_WINDOWS = (2, 4, 8, 16)
POOL_WIDTH = 256
POOL_GROUP = 64
DIFF_WIDTH = 512
DIFF_HEAD_DIM = 64
DIFF_HEADS = 4
SB_WIDTH = 256
SB_HEAD_DIM = 64
SB_HEADS = 4
D_IN = POOL_WIDTH + 3 * DIFF_WIDTH + 3 * SB_WIDTH
D_FF = 2816
EPS = 1e-6

LANES = 128
N_SLABS = (D_IN - POOL_WIDTH) // LANES
SB_SLABS = SB_WIDTH // LANES
BLK_DQ, BLK_DK = 0, 1
SLAB_DV = 2 * DIFF_HEADS
BLK_SQ, BLK_SK, BLK_SV = 6, 7, 8

TM_PROJ = 512
TM_FFN = 512
POOL_HALO = 16
TQ = 256
TK = 256
FF_CHUNK = 512
VMEM_LIMIT = 56 * 1024 * 1024
DIFF_VDIM = 2 * DIFF_HEAD_DIM
DIFF_Q_SCALE =DIFF_HEAD_DIM ** -0.5 * math.log2(math.e)
SB_DEAD_MASS = 105.0

_NT = (((1,), (1,)), ((), ()))


def _resident(shape):
    return pl.BlockSpec(shape, lambda *_: (0,) * len(shape), pipeline_mode=pl.Buffered(1))


def _layer_weight(shape, layer):
    return pl.BlockSpec((None,) + shape, lambda *_: (layer, 0, 0), pipeline_mode=pl.Buffered(1))


def _rms(x, g):
    ms = jnp.mean(x * x, axis=-1, keepdims=True)
    return x * lax.rsqrt(ms + EPS) * g


def _pool_mix(e0, first_pos, w_blockdiag, scale):
    s2 = e0 + pltpu.roll(e0, 1, 0)
    s4 = s2 + pltpu.roll(s2, 2, 0)
    s8 = s4 + pltpu.roll(s4, 4, 0)
    s16 = s8 + pltpu.roll(s8, 8, 0)
    shape = e0.shape
    lane = lax.broadcasted_iota(jnp.int32, shape, 1)
    row = lax.broadcasted_iota(jnp.int32, shape, 0)
    g0, g1, g2 = lane < POOL_GROUP, lane < 2 * POOL_GROUP, lane < 3 * POOL_GROUP
    win = jnp.where(g0, s2, jnp.where(g1, s4, jnp.where(g2, s8, s16)))
    width = jnp.where(g0, POOL_WINDOWS[0],
                      jnp.where(g1, POOL_WINDOWS[1],
                                jnp.where(g2, POOL_WINDOWS[2], POOL_WINDOWS[3])))
    pos = first_pos + row - POOL_HALO
    cnt = jnp.maximum(jnp.minimum(pos + 1, width), 1).astype(F32)
    pooled = (win / cnt - e0)[POOL_HALO:]
    mixed = jnp.dot(pooled.astype(BF16), w_blockdiag, preferred_element_type=F32)
    return (mixed * scale).astype(BF16)


def _proj_kernel(x_ref, g_ref, w_ref, wp_ref, sp_ref, a_ref, p_ref, vt_ref, halo_ref,
                 *, steps_per_seq):
    i = pl.program_id(0) % steps_per_seq

    @pl.when(i == 0)
    def _():
        halo_ref[...] = jnp.zeros(halo_ref.shape, F32)

    h = _rms(x_ref[...], g_ref[...]).astype(BF16)
    u = jnp.dot(h, w_ref[:, :POOL_WIDTH], preferred_element_type=F32)
    for c0 in range(0, N_SLABS, 4):
        lo = POOL_WIDTH + c0 * LANES
        n = min(4, N_SLABS - c0)
        acc = jnp.dot(h, w_ref[:, lo:lo + n * LANES], preferred_element_type=F32)
        for j in range(n):
            slab = acc[:, j * LANES:(j + 1) * LANES]
            if c0 + j < DIFF_HEADS:
                slab = slab * DIFF_Q_SCALE
            p_ref[c0 + j] = slab.astype(BF16)
            head = c0 + j - SLAB_DV
            if 0 <= head < DIFF_HEADS:
                for c in range(TM_PROJ // TK):
                    vt_ref[head, c] = slab[c * TK:(c + 1) * TK, :].T.astype(BF16)
    e0 = jnp.concatenate([halo_ref[...], u], axis=0)
    a_ref[...] = _pool_mix(e0, i * TM_PROJ, wp_ref[...], sp_ref[...])
    halo_ref[...] = u[TM_PROJ - POOL_HALO:]


def _proj(x2d, g, w_stack, layer, w_blockdiag, pool_scale, seq):
    t = x2d.shape[0]
    return pl.pallas_call(
        functools.partial(_proj_kernel, steps_per_seq=seq // TM_PROJ),
        grid=(t // TM_PROJ,),
        in_specs=[
            pl.BlockSpec((TM_PROJ, D_MODEL), lambda i: (i, 0)),
            _resident((1, D_MODEL)),
            _layer_weight((D_MODEL, D_IN), layer),
            _resident((POOL_WIDTH, POOL_WIDTH)),
            _resident((1, POOL_WIDTH)),
        ],
        out_specs=[
            pl.BlockSpec((TM_PROJ, POOL_WIDTH), lambda i: (i, 0)),
            pl.BlockSpec((N_SLABS, TM_PROJ, LANES), lambda i: (0, i, 0)),
            pl.BlockSpec((DIFF_HEADS, TM_PROJ // TK, DIFF_VDIM, TK), lambda i: (0, i, 0, 0)),
        ],
        out_shape=[
            jax.ShapeDtypeStruct((t, POOL_WIDTH), BF16),
            jax.ShapeDtypeStruct((N_SLABS, t, LANES), BF16),
            jax.ShapeDtypeStruct((DIFF_HEADS, t // TK, DIFF_VDIM, TK), BF16),
        ],
        scratch_shapes=[pltpu.VMEM((POOL_HALO, POOL_WIDTH), F32)],
        compiler_params=pltpu.CompilerParams(
            dimension_semantics=("arbitrary",), vmem_limit_bytes=VMEM_LIMIT),
        name="proj",
    )(x2d, g.reshape(1, D_MODEL), w_stack, w_blockdiag, pool_scale.reshape(1, POOL_WIDTH))


def _twice(a):
    return jnp.concatenate([a] * (TK // LANES), axis=1)


def _diff_kernel(q_ref, k_ref, vt_ref, lamp_ref, g_ref, o_ref,
                 m_ref, l_ref, acc_ref, s0_ref, s1_ref, *, lam_init):
    qi = pl.program_id(1)
    lane = lax.broadcasted_iota(jnp.int32, (TQ, LANES), 1)
    first_map = lane < DIFF_HEAD_DIM
    qms = []
    for h in range(DIFF_HEADS):
        q = q_ref[h]
        zero = jnp.zeros_like(q)
        qms += [jnp.where(first_map, q, zero), jnp.where(first_map, zero, q)]
    key = lax.broadcasted_iota(jnp.int32, (TK, TQ), 0)
    qry = lax.broadcasted_iota(jnp.int32, (TK, TQ), 1)
    causal = key <= qry

    s_refs = (s0_ref, s1_ref)

    def score_tile(kb, slot):
        start = pl.multiple_of(kb * TK, TK)
        for i, qm in enumerate(qms):
            s_refs[slot][i] = lax.dot_general(k_ref[i // 2, pl.ds(start, TK), :], qm, _NT,
                                              preferred_element_type=F32)

    def softmax_tile(kb, slot, diagonal):
        probs, alphas = [], []
        for i in range(len(qms)):
            s = s_refs[slot][i]
            if diagonal:
                s = jnp.where(causal, s, -jnp.inf)
            m_prev = m_ref[i]
            m_new = jnp.maximum(m_prev, jnp.max(s, axis=0, keepdims=True))
            alphas.append(jnp.exp2(m_prev - m_new))
            p = jnp.exp2(s - m_new)
            l_ref[i] = alphas[i] * l_ref[i] + jnp.sum(p, axis=0, keepdims=True)
            m_ref[i] = m_new
            probs.append(p.astype(BF16))
        for i, p in enumerate(probs):
            pv = jnp.dot(vt_ref[i // 2, kb], p, preferred_element_type=F32)
            acc_ref[i] = alphas[i] * acc_ref[i] + pv

    m_ref[...] = jnp.full(m_ref.shape, -jnp.inf, F32)
    l_ref[...] = jnp.zeros(l_ref.shape, F32)
    acc_ref[...] = jnp.zeros(acc_ref.shape, F32)
    score_tile(qi, 1)
    score_tile(0, 0)
    softmax_tile(qi, 1, True)

    def run(kb, count):
        for t in range(count):
            score_tile(jnp.minimum(kb + t + 1, qi), (t + 1) % 2)
            softmax_tile(kb + t, t % 2, False)

    def quad(jj, carry):
        run(4 * jj, 4)
        return carry

    lax.fori_loop(0, qi // 4, quad, 0)
    rest = (qi // 4) * 4
    pl.when(qi % 4 >= 2)(functools.partial(run, rest, 2))
    pl.when(qi % 2 == 1)(lambda: softmax_tile(qi - 1, 0, False))

    lamp = lamp_ref[...]
    lam = (jnp.exp(jnp.sum(lamp[0:1] * lamp[1:2], axis=1, keepdims=True))
           - jnp.exp(jnp.sum(lamp[2:3] * lamp[3:4], axis=1, keepdims=True)) + lam_init)
    for h in range(DIFF_HEADS):
        o = acc_ref[2 * h] / l_ref[2 * h] - lam * (acc_ref[2 * h + 1] / l_ref[2 * h + 1])
        ms = jnp.mean(o * o, axis=0, keepdims=True)
        y = o * lax.rsqrt(ms + EPS) * g_ref[...] * (1.0 - lam_init)
        o_ref[:, h * LANES:(h + 1) * LANES] = y.T.astype(BF16)


def _diff_attention(p3, vt, lamp, g, lam_init, batch, seq):
    t = p3.shape[1]
    nq = seq // TQ
    nk = seq // TK
    return pl.pallas_call(
        functools.partial(_diff_kernel, lam_init=lam_init),
        grid=(batch, nq),
        in_specs=[
            pl.BlockSpec((DIFF_HEADS, TQ, LANES), lambda b, i: (BLK_DQ, b * nq + i, 0)),
            pl.BlockSpec((DIFF_HEADS, seq, LANES), lambda b, i: (BLK_DK, b, 0)),
            pl.BlockSpec((DIFF_HEADS, nk, DIFF_VDIM, TK), lambda b, i: (0, b, 0, 0)),
            _resident((4, DIFF_HEAD_DIM)),
            _resident((DIFF_VDIM, 1)),
        ],
        out_specs=pl.BlockSpec((TQ, DIFF_WIDTH), lambda b, i: (b * nq + i, 0)),
        out_shape=jax.ShapeDtypeStruct((t, DIFF_WIDTH), BF16),
        scratch_shapes=[
            pltpu.VMEM((2 * DIFF_HEADS, 1, TQ), F32),
            pltpu.VMEM((2 * DIFF_HEADS, 1, TQ), F32),
            pltpu.VMEM((2 * DIFF_HEADS, DIFF_VDIM, TQ), F32),
            pltpu.VMEM((2 * DIFF_HEADS, TK, TQ), F32),
            pltpu.VMEM((2 * DIFF_HEADS, TK, TQ), F32),
        ],
        compiler_params=pltpu.CompilerParams(
            dimension_semantics=("arbitrary",) * 2, vmem_limit_bytes=VMEM_LIMIT),
        name="diff_attn",
    )(p3, p3, vt, lamp, g.reshape(DIFF_VDIM, 1))


def _sb_kernel(q_ref, k_ref, v_ref, g_ref, o_ref, tri_ref, carry_ref, acc_ref, z0_ref, z1_ref):
    b, qi = pl.program_id(0), pl.program_id(1)

    @pl.when((b == 0) & (qi == 0))
    def _():
        j = lax.broadcasted_iota(jnp.int32, (2 * TK, TK), 0)
        s = lax.broadcasted_iota(jnp.int32, (2 * TK, TK), 1)
        j = jnp.where(j >= TK, j - TK, j)
        tri_ref[...] = jnp.where(j >= s, 1.0, 0.0).astype(BF16)

    scale = jnp.asarray(SB_HEAD_DIM ** -0.5, BF16)
    lane = lax.broadcasted_iota(jnp.int32, (TQ, LANES), 1)
    lower = lane < SB_HEAD_DIM
    qms = []
    for sl in range(SB_SLABS):
        q = q_ref[sl] * scale
        zero = jnp.zeros_like(q)
        qms += [jnp.where(lower, q, zero), jnp.where(lower, zero, q)]
    row = lax.broadcasted_iota(jnp.int32, (TQ, TK), 0)
    col = lax.broadcasted_iota(jnp.int32, (TQ, TK), 1)
    strict = col < row

    z_refs = (z0_ref, z1_ref)

    def score_tile(kb, slot):
        start = pl.multiple_of(kb * TK, TK)
        for hd, qm in enumerate(qms):
            z_refs[slot][hd] = lax.dot_general(qm, k_ref[hd // 2, pl.ds(start, TK), :], _NT,
                                               preferred_element_type=F32)

    def weight_tile(kb, slot, diagonal):
        start = pl.multiple_of(kb * TK, TK)
        zs = [z_refs[slot][hd] for hd in range(SB_HEADS)]
        splits = []
        for z in zs:
            sp = jnp.maximum(z, 0.0) + jnp.log(1.0 + jnp.exp(-jnp.abs(z)))
            if diagonal:
                sp = jnp.where(strict, sp, 0.0)
            hi = sp.astype(BF16)
            lo = (sp - hi.astype(F32)).astype(BF16)
            splits.append(jnp.concatenate([hi, lo], axis=1))
        sums = [jnp.dot(s, tri_ref[...], preferred_element_type=F32) for s in splits]
        weights = []
        for hd, (z, c) in enumerate(zip(zs, sums)):
            if not diagonal:
                c = c + _twice(carry_ref[hd])
            a = jnp.exp(z - c)
            if diagonal:
                a = jnp.where(strict, a, 0.0)
            carry_ref[hd] = jnp.broadcast_to(c[:, 0:1], (TQ, LANES))
            weights.append(a.astype(BF16))
        for hd, a in enumerate(weights):
            pv = jnp.dot(a, v_ref[hd // 2, pl.ds(start, TK), :], preferred_element_type=F32)
            acc_ref[hd] = pv if diagonal else acc_ref[hd] + pv

    score_tile(qi, 0)
    score_tile(jnp.maximum(qi - 1, 0), 1)
    weight_tile(qi, 0, True)

    def step(j, slot):
        kb = qi - 1 - j
        score_tile(jnp.maximum(kb - 1, 0), 1 - slot)
        weight_tile(kb, slot, False)

    def open_rows():
        return (jnp.min(carry_ref[...]) < SB_DEAD_MASS).astype(jnp.int32)

    def body(state):
        j, _ = state
        for slot in range(2):
            pl.when((j + 1) % 2 == slot)(functools.partial(step, j, slot))
        return j + 1, open_rows()

    lax.while_loop(lambda state: (state[0] < qi) & (state[1] > 0), body,
                   (jnp.int32(0), open_rows()))

    lane_o = lax.broadcasted_iota(jnp.int32, (TQ, LANES), 1)
    first = lane_o < SB_HEAD_DIM
    for sl in range(SB_SLABS):
        o = jnp.where(first, acc_ref[2 * sl], acc_ref[2 * sl + 1])
        sq = o * o
        ms = jnp.where(first,
                       jnp.sum(jnp.where(first, sq, 0.0), axis=1, keepdims=True),
                       jnp.sum(jnp.where(first, 0.0, sq), axis=1, keepdims=True)) / SB_HEAD_DIM
        o_ref[:, sl * LANES:(sl + 1) * LANES] = (
            o * lax.rsqrt(ms + EPS) * g_ref[...]).astype(BF16)


def _sb_attention(p3, g, batch, seq):
    t = p3.shape[1]
    nq = seq // TQ
    return pl.pallas_call(
        _sb_kernel,
        grid=(batch, nq),
        in_specs=[
            pl.BlockSpec((SB_SLABS, TQ, LANES), lambda b, i: (BLK_SQ, b * nq + i, 0)),
            pl.BlockSpec((SB_SLABS, seq, LANES), lambda b, i: (BLK_SK, b, 0)),
            pl.BlockSpec((SB_SLABS, seq, LANES), lambda b, i: (BLK_SV, b, 0)),
            _resident((1, LANES)),
        ],
        out_specs=pl.BlockSpec((TQ, SB_WIDTH), lambda b, i: (b * nq + i, 0)),
        out_shape=jax.ShapeDtypeStruct((t, SB_WIDTH), BF16),
        scratch_shapes=[
            pltpu.VMEM((2 * TK, TK), BF16),
            pltpu.VMEM((SB_HEADS, TQ, LANES), F32),
            pltpu.VMEM((SB_HEADS, TQ, LANES), F32),
            pltpu.VMEM((SB_HEADS, TQ, TK), F32),
            pltpu.VMEM((SB_HEADS, TQ, TK), F32),
        ],
        compiler_params=pltpu.CompilerParams(
            dimension_semantics=("arbitrary",) * 2, vmem_limit_bytes=VMEM_LIMIT),
        name="sb_attn",
    )(p3, p3, p3, jnp.concatenate([g, g]).reshape(1, LANES))


def _ffn_kernel(x_ref, a_ref, b_ref, c_ref, wo_ref, g2_ref, wg_ref, wu_ref, wd_ref, gf_ref,
                o_ref, x1_ref, act_ref, *, last):
    mix = jnp.concatenate([a_ref[...], b_ref[...], c_ref[...]], axis=1)
    x1_ref[...] = x_ref[...] + jnp.dot(mix, wo_ref[...], preferred_element_type=F32)
    h2 = _rms(x1_ref[...], g2_ref[...]).astype(BF16)
    for f0 in range(0, D_FF, FF_CHUNK):
        f1 = min(f0 + FF_CHUNK, D_FF)
        gate = jnp.dot(h2, wg_ref[:, f0:f1], preferred_element_type=F32)
        up = jnp.dot(h2, wu_ref[:, f0:f1], preferred_element_type=F32)
        act_ref[:, f0:f1] = (gate * (1.0 / (1.0 + jnp.exp(-gate))) * up).astype(BF16)
    x2 = x1_ref[...] + jnp.dot(act_ref[...], wd_ref[...], preferred_element_type=F32)
    o_ref[...] = _rms(x2, gf_ref[...]) if last else x2


def _ffn(x2d, a, b, c, wo, g2, wg, wu, wd, gf, layer, last):
    t = x2d.shape[0]
    rows = lambda w: pl.BlockSpec((TM_FFN, w), lambda i: (i, 0))
    return pl.pallas_call(
        functools.partial(_ffn_kernel, last=last),
        grid=(t // TM_FFN,),
        in_specs=[
            rows(D_MODEL), rows(POOL_WIDTH), rows(DIFF_WIDTH), rows(SB_WIDTH),
            _layer_weight((D_MODEL, D_MODEL), layer), _resident((1, D_MODEL)),
            _layer_weight((D_MODEL, D_FF), layer), _layer_weight((D_MODEL, D_FF), layer),
            _layer_weight((D_FF, D_MODEL), layer),
            _resident((1, D_MODEL)),
        ],
        out_specs=rows(D_MODEL),
        out_shape=jax.ShapeDtypeStruct((t, D_MODEL), F32),
        scratch_shapes=[
            pltpu.VMEM((TM_FFN, D_MODEL), F32),
            pltpu.VMEM((TM_FFN, D_FF), BF16),
        ],
        compiler_params=pltpu.CompilerParams(
            dimension_semantics=("arbitrary",), vmem_limit_bytes=VMEM_LIMIT),
        name="ffn_last" if last else "ffn",
    )(x2d, a, b, c, wo, g2.reshape(1, D_MODEL), wg, wu, wd, gf.reshape(1, D_MODEL))


def kernel(x, norm1_g, w_in, pool_w, pool_scale, lam_q1, lam_k1, lam_q2, lam_k2,
           diff_norm_g, sb_norm_g, w_out, norm2_g, w_gate, w_up, w_down, final_norm_g):
    batch, seq, d = x.shape
    depth = w_in.shape[0]
    assert d == D_MODEL and seq % TQ == 0 and seq % TM_PROJ == 0
    assert (batch * seq) % TM_FFN == 0

    w_in_b, w_out_b = w_in.astype(BF16), w_out.astype(BF16)
    w_gate_b, w_up_b, w_down_b = w_gate.astype(BF16), w_up.astype(BF16), w_down.astype(BF16)
    x2d = x.reshape(batch * seq, d)
    for l in range(depth):
        w_bd = jax.scipy.linalg.block_diag(*[pool_w[l, gi] for gi in range(len(POOL_WINDOWS))])
        a, p3, vt = _proj(x2d, norm1_g[l], w_in_b, l, w_bd.astype(BF16), pool_scale[l], seq)
        lam_init = 0.8 - 0.6 * math.exp(-0.3 * l)
        lamp = jnp.stack([lam_q1[l], lam_k1[l], lam_q2[l], lam_k2[l]])
        bo = _diff_attention(p3, vt, lamp, diff_norm_g[l], lam_init, batch, seq)
        co = _sb_attention(p3, sb_norm_g[l], batch, seq)
        x2d = _ffn(x2d, a, bo, co, w_out_b, norm2_g[l], w_gate_b, w_up_b, w_down_b,
                   final_norm_g, l, last=(l == depth - 1))
    return x2d.reshape(batch, seq, d)
```

```python
import functools
import math

import jax
import jax.numpy as jnp
from jax import lax
from jax.experimental import pallas as pl
from jax.experimental.pallas import tpu as pltpu

F32 = jnp.float32
BF16 = jnp.bfloat16

D_MODEL = 1024
POOL_WINDOWS = (2, 4, 8, 16)
POOL_WIDTH = 256
POOL_GROUP = 64
DIFF_WIDTH = 512
DIFF_HEAD_DIM = 64
DIFF_HEADS = 4
SB_WIDTH = 256
SB_HEAD_DIM = 64
SB_HEADS = 4
D_IN = POOL_WIDTH + 3 * DIFF_WIDTH + 3 * SB_WIDTH
D_FF = 2816
EPS = 1e-6

LANES = 128
N_SLABS = (D_IN - POOL_WIDTH) // LANES
SB_SLABS = SB_WIDTH // LANES
BLK_DQ, BLK_DK = 0, 1
SLAB_DV = 2 * DIFF_HEADS
BLK_SQ, BLK_SK, BLK_SV = 6, 7, 8

TM_PROJ = 512
TM_FFN = 512
POOL_HALO = 16
TQ = 256
TK = 256
FF_CHUNK = 512
VMEM_LIMIT = 56 * 1024 * 1024
DIFF_VDIM = 2 * DIFF_HEAD_DIM
DIFF_Q_SCALE =DIFF_HEAD_DIM ** -0.5 * math.log2(math.e)
SB_DEAD_MASS = 105.0

_NT = (((1,), (1,)), ((), ()))


def _resident(shape):
    return pl.BlockSpec(shape, lambda *_: (0,) * len(shape), pipeline_mode=pl.Buffered(1))


def _layer_weight(shape, layer):
    return pl.BlockSpec((None,) + shape, lambda *_: (layer, 0, 0), pipeline_mode=pl.Buffered(1))


def _rms(x, g):
    ms = jnp.mean(x * x, axis=-1, keepdims=True)
    return x * lax.rsqrt(ms + EPS) * g


def _pool_mix(e0, first_pos, w_blockdiag, scale):
    s2 = e0 + pltpu.roll(e0, 1, 0)
    s4 = s2 + pltpu.roll(s2, 2, 0)
    s8 = s4 + pltpu.roll(s4, 4, 0)
    s16 = s8 + pltpu.roll(s8, 8, 0)
    shape = e0.shape
    lane = lax.broadcasted_iota(jnp.int32, shape, 1)
    row = lax.broadcasted_iota(jnp.int32, shape, 0)
    g0, g1, g2 = lane < POOL_GROUP, lane < 2 * POOL_GROUP, lane < 3 * POOL_GROUP
    win = jnp.where(g0, s2, jnp.where(g1, s4, jnp.where(g2, s8, s16)))
    width = jnp.where(g0, POOL_WINDOWS[0],
                      jnp.where(g1, POOL_WINDOWS[1],
                                jnp.where(g2, POOL_WINDOWS[2], POOL_WINDOWS[3])))
    pos = first_pos + row - POOL_HALO
    cnt = jnp.maximum(jnp.minimum(pos + 1, width), 1).astype(F32)
    pooled = (win / cnt - e0)[POOL_HALO:]
    mixed = jnp.dot(pooled.astype(BF16), w_blockdiag, preferred_element_type=F32)
    return (mixed * scale).astype(BF16)


def _proj_kernel(x_ref, g_ref, w_ref, wp_ref, sp_ref, a_ref, p_ref, vt_ref, halo_ref,
                 *, steps_per_seq):
    i = pl.program_id(0) % steps_per_seq

    @pl.when(i == 0)
    def _():
        halo_ref[...] = jnp.zeros(halo_ref.shape, F32)

    h = _rms(x_ref[...], g_ref[...]).astype(BF16)
    u = jnp.dot(h, w_ref[:, :POOL_WIDTH], preferred_element_type=F32)
    for c0 in range(0, N_SLABS, 4):
        lo = POOL_WIDTH + c0 * LANES
        n = min(4, N_SLABS - c0)
        acc = jnp.dot(h, w_ref[:, lo:lo + n * LANES], preferred_element_type=F32)
        for j in range(n):
            slab = acc[:, j * LANES:(j + 1) * LANES]
            if c0 + j < DIFF_HEADS:
                slab = slab * DIFF_Q_SCALE
            p_ref[c0 + j] = slab.astype(BF16)
            head = c0 + j - SLAB_DV
            if 0 <= head < DIFF_HEADS:
                for c in range(TM_PROJ // TK):
                    vt_ref[head, c] = slab[c * TK:(c + 1) * TK, :].T.astype(BF16)
    e0 = jnp.concatenate([halo_ref[...], u], axis=0)
    a_ref[...] = _pool_mix(e0, i * TM_PROJ, wp_ref[...], sp_ref[...])
    halo_ref[...] = u[TM_PROJ - POOL_HALO:]


def _proj(x2d, g, w_stack, layer, w_blockdiag, pool_scale, seq):
    t = x2d.shape[0]
    return pl.pallas_call(
        functools.partial(_proj_kernel, steps_per_seq=seq // TM_PROJ),
        grid=(t // TM_PROJ,),
        in_specs=[
            pl.BlockSpec((TM_PROJ, D_MODEL), lambda i: (i, 0)),
            _resident((1, D_MODEL)),
            _layer_weight((D_MODEL, D_IN), layer),
            _resident((POOL_WIDTH, POOL_WIDTH)),
            _resident((1, POOL_WIDTH)),
        ],
        out_specs=[
            pl.BlockSpec((TM_PROJ, POOL_WIDTH), lambda i: (i, 0)),
            pl.BlockSpec((N_SLABS, TM_PROJ, LANES), lambda i: (0, i, 0)),
            pl.BlockSpec((DIFF_HEADS, TM_PROJ // TK, DIFF_VDIM, TK), lambda i: (0, i, 0, 0)),
        ],
        out_shape=[
            jax.ShapeDtypeStruct((t, POOL_WIDTH), BF16),
            jax.ShapeDtypeStruct((N_SLABS, t, LANES), BF16),
            jax.ShapeDtypeStruct((DIFF_HEADS, t // TK, DIFF_VDIM, TK), BF16),
        ],
        scratch_shapes=[pltpu.VMEM((POOL_HALO, POOL_WIDTH), F32)],
        compiler_params=pltpu.CompilerParams(
            dimension_semantics=("arbitrary",), vmem_limit_bytes=VMEM_LIMIT),
        name="proj",
    )(x2d, g.reshape(1, D_MODEL), w_stack, w_blockdiag, pool_scale.reshape(1, POOL_WIDTH))


def _twice(a):
    return jnp.concatenate([a] * (TK // LANES), axis=1)


def _diff_kernel(q_ref, k_ref, vt_ref, lamp_ref, g_ref, o_ref,
                 m_ref, l_ref, acc_ref, s0_ref, s1_ref, *, lam_init):
    qi = pl.program_id(1)
    lane = lax.broadcasted_iota(jnp.int32, (TQ, LANES), 1)
    first_map = lane < DIFF_HEAD_DIM
    qms = []
    for h in range(DIFF_HEADS):
        q = q_ref[h]
        zero = jnp.zeros_like(q)
        qms += [jnp.where(first_map, q, zero), jnp.where(first_map, zero, q)]
    key = lax.broadcasted_iota(jnp.int32, (TK, TQ), 0)
    qry = lax.broadcasted_iota(jnp.int32, (TK, TQ), 1)
    causal = key <= qry

    s_refs = (s0_ref, s1_ref)

    def score_tile(kb, slot):
        start = pl.multiple_of(kb * TK, TK)
        for i, qm in enumerate(qms):
            s_refs[slot][i] = lax.dot_general(k_ref[i // 2, pl.ds(start, TK), :], qm, _NT,
                                              preferred_element_type=F32)

    def softmax_tile(kb, slot, diagonal, next_kb=None):
        if next_kb is not None:
            next_start = pl.multiple_of(next_kb * TK, TK)
        for i, qm in enumerate(qms):
            if next_kb is not None:
                s_refs[1 - slot][i] = lax.dot_general(
                    k_ref[i // 2, pl.ds(next_start, TK), :], qm, _NT, preferred_element_type=F32)
            s = s_refs[slot][i]
            if diagonal:
                s = jnp.where(causal, s, -jnp.inf)
            m_prev = m_ref[i]
            m_new = jnp.maximum(m_prev, jnp.max(s, axis=0, keepdims=True))
            alpha = jnp.exp2(m_prev - m_new)
            p = jnp.exp2(s - m_new)
            l_ref[i] = alpha * l_ref[i] + jnp.sum(p, axis=0, keepdims=True)
            m_ref[i] = m_new
            pv = jnp.dot(vt_ref[i // 2, kb], p.astype(BF16), preferred_element_type=F32)
            acc_ref[i] = alpha * acc_ref[i] + pv

    m_ref[...] = jnp.full(m_ref.shape, -jnp.inf, F32)
    l_ref[...] = jnp.zeros(l_ref.shape, F32)
    acc_ref[...] = jnp.zeros(acc_ref.shape, F32)
    score_tile(qi, 1)
    softmax_tile(qi, 1, True, next_kb=0)

    def run(kb, count):
        for t in range(count):
            softmax_tile(kb + t, t % 2, False, next_kb=jnp.minimum(kb + t + 1, qi))

    def quad(jj, carry):
        run(4 * jj, 4)
        return carry

    lax.fori_loop(0, qi // 4, quad, 0)
    rest = (qi // 4) * 4
    pl.when(qi % 4 >= 2)(functools.partial(run, rest, 2))
    pl.when(qi % 2 == 1)(lambda: softmax_tile(qi - 1, 0, False))

    lamp = lamp_ref[...]
    lam = (jnp.exp(jnp.sum(lamp[0:1] * lamp[1:2], axis=1, keepdims=True))
           - jnp.exp(jnp.sum(lamp[2:3] * lamp[3:4], axis=1, keepdims=True)) + lam_init)
    for h in range(DIFF_HEADS):
        o = acc_ref[2 * h] / l_ref[2 * h] - lam * (acc_ref[2 * h + 1] / l_ref[2 * h + 1])
        ms = jnp.mean(o * o, axis=0, keepdims=True)
        y = o * lax.rsqrt(ms + EPS) * g_ref[...] * (1.0 - lam_init)
        o_ref[:, h * LANES:(h + 1) * LANES] = y.T.astype(BF16)


def _diff_attention(p3, vt, lamp, g, lam_init, batch, seq):
    t = p3.shape[1]
    nq = seq // TQ
    nk = seq // TK
    return pl.pallas_call(
        functools.partial(_diff_kernel, lam_init=lam_init),
        grid=(batch, nq),
        in_specs=[
            pl.BlockSpec((DIFF_HEADS, TQ, LANES), lambda b, i: (BLK_DQ, b * nq + i, 0)),
            pl.BlockSpec((DIFF_HEADS, seq, LANES), lambda b, i: (BLK_DK, b, 0)),
            pl.BlockSpec((DIFF_HEADS, nk, DIFF_VDIM, TK), lambda b, i: (0, b, 0, 0)),
            _resident((4, DIFF_HEAD_DIM)),
            _resident((DIFF_VDIM, 1)),
        ],
        out_specs=pl.BlockSpec((TQ, DIFF_WIDTH), lambda b, i: (b * nq + i, 0)),
        out_shape=jax.ShapeDtypeStruct((t, DIFF_WIDTH), BF16),
        scratch_shapes=[
            pltpu.VMEM((2 * DIFF_HEADS, 1, TQ), F32),
            pltpu.VMEM((2 * DIFF_HEADS, 1, TQ), F32),
            pltpu.VMEM((2 * DIFF_HEADS, DIFF_VDIM, TQ), F32),
            pltpu.VMEM((2 * DIFF_HEADS, TK, TQ), F32),
            pltpu.VMEM((2 * DIFF_HEADS, TK, TQ), F32),
        ],
        compiler_params=pltpu.CompilerParams(
            dimension_semantics=("arbitrary",) * 2, vmem_limit_bytes=VMEM_LIMIT),
        name="diff_attn",
    )(p3, p3, vt, lamp, g.reshape(DIFF_VDIM, 1))


def _sb_kernel(q_ref, k_ref, v_ref, g_ref, o_ref, tri_ref, carry_ref, acc_ref, z0_ref, z1_ref):
    b, qi = pl.program_id(0), pl.program_id(1)

    @pl.when((b == 0) & (qi == 0))
    def _():
        j = lax.broadcasted_iota(jnp.int32, (2 * TK, TK), 0)
        s = lax.broadcasted_iota(jnp.int32, (2 * TK, TK), 1)
        j = jnp.where(j >= TK, j - TK, j)
        tri_ref[...] = jnp.where(j >= s, 1.0, 0.0).astype(BF16)

    scale = jnp.asarray(SB_HEAD_DIM ** -0.5, BF16)
    lane = lax.broadcasted_iota(jnp.int32, (TQ, LANES), 1)
    lower = lane < SB_HEAD_DIM
    qms = []
    for sl in range(SB_SLABS):
        q = q_ref[sl] * scale
        zero = jnp.zeros_like(q)
        qms += [jnp.where(lower, q, zero), jnp.where(lower, zero, q)]
    row = lax.broadcasted_iota(jnp.int32, (TQ, TK), 0)
    col = lax.broadcasted_iota(jnp.int32, (TQ, TK), 1)
    strict = col < row

    z_refs = (z0_ref, z1_ref)

    def score_tile(kb, slot):
        start = pl.multiple_of(kb * TK, TK)
        for hd, qm in enumerate(qms):
            z_refs[slot][hd] = lax.dot_general(qm, k_ref[hd // 2, pl.ds(start, TK), :], _NT,
                                               preferred_element_type=F32)

    def weight_tile(kb, slot, diagonal):
        start = pl.multiple_of(kb * TK, TK)
        zs = [z_refs[slot][hd] for hd in range(SB_HEADS)]
        splits = []
        for z in zs:
            sp = jnp.maximum(z, 0.0) + jnp.log(1.0 + jnp.exp(-jnp.abs(z)))
            if diagonal:
                sp = jnp.where(strict, sp, 0.0)
            hi = sp.astype(BF16)
            lo = (sp - hi.astype(F32)).astype(BF16)
            splits.append(jnp.concatenate([hi, lo], axis=1))
        sums = [jnp.dot(s, tri_ref[...], preferred_element_type=F32) for s in splits]
        weights = []
        for hd, (z, c) in enumerate(zip(zs, sums)):
            if not diagonal:
                c = c + _twice(carry_ref[hd])
            a = jnp.exp(z - c)
            if diagonal:
                a = jnp.where(strict, a, 0.0)
            carry_ref[hd] = jnp.broadcast_to(c[:, 0:1], (TQ, LANES))
            weights.append(a.astype(BF16))
        for hd, a in enumerate(weights):
            pv = jnp.dot(a, v_ref[hd // 2, pl.ds(start, TK), :], preferred_element_type=F32)
            acc_ref[hd] = pv if diagonal else acc_ref[hd] + pv

    score_tile(qi, 0)
    score_tile(jnp.maximum(qi - 1, 0), 1)
    weight_tile(qi, 0, True)

    def step(j, slot):
        kb = qi - 1 - j
        score_tile(jnp.maximum(kb - 1, 0), 1 - slot)
        weight_tile(kb, slot, False)

    def open_rows():
        return (jnp.min(carry_ref[...]) < SB_DEAD_MASS).astype(jnp.int32)

    def body(state):
        j, _ = state
        for slot in range(2):
            pl.when((j + 1) % 2 == slot)(functools.partial(step, j, slot))
        return j + 1, open_rows()

    lax.while_loop(lambda state: (state[0] < qi) & (state[1] > 0), body,
                   (jnp.int32(0), open_rows()))

    lane_o = lax.broadcasted_iota(jnp.int32, (TQ, LANES), 1)
    first = lane_o < SB_HEAD_DIM
    for sl in range(SB_SLABS):
        o = jnp.where(first, acc_ref[2 * sl], acc_ref[2 * sl + 1])
        sq = o * o
        ms = jnp.where(first,
                       jnp.sum(jnp.where(first, sq, 0.0), axis=1, keepdims=True),
                       jnp.sum(jnp.where(first, 0.0, sq), axis=1, keepdims=True)) / SB_HEAD_DIM
        o_ref[:, sl * LANES:(sl + 1) * LANES] = (
            o * lax.rsqrt(ms + EPS) * g_ref[...]).astype(BF16)


def _sb_attention(p3, g, batch, seq):
    t = p3.shape[1]
    nq = seq // TQ
    return pl.pallas_call(
        _sb_kernel,
        grid=(batch, nq),
        in_specs=[
            pl.BlockSpec((SB_SLABS, TQ, LANES), lambda b, i: (BLK_SQ, b * nq + i, 0)),
            pl.BlockSpec((SB_SLABS, seq, LANES), lambda b, i: (BLK_SK, b, 0)),
            pl.BlockSpec((SB_SLABS, seq, LANES), lambda b, i: (BLK_SV, b, 0)),
            _resident((1, LANES)),
        ],
        out_specs=pl.BlockSpec((TQ, SB_WIDTH), lambda b, i: (b * nq + i, 0)),
        out_shape=jax.ShapeDtypeStruct((t, SB_WIDTH), BF16),
        scratch_shapes=[
            pltpu.VMEM((2 * TK, TK), BF16),
            pltpu.VMEM((SB_HEADS, TQ, LANES), F32),
            pltpu.VMEM((SB_HEADS, TQ, LANES), F32),
            pltpu.VMEM((SB_HEADS, TQ, TK), F32),
            pltpu.VMEM((SB_HEADS, TQ, TK), F32),
        ],
        compiler_params=pltpu.CompilerParams(
            dimension_semantics=("arbitrary",) * 2, vmem_limit_bytes=VMEM_LIMIT),
        name="sb_attn",
    )(p3, p3, p3, jnp.concatenate([g, g]).reshape(1, LANES))


def _ffn_kernel(x_ref, a_ref, b_ref, c_ref, wo_ref, g2_ref, wg_ref, wu_ref, wd_ref, gf_ref,
                o_ref, x1_ref, act_ref, *, last):
    mix = jnp.concatenate([a_ref[...], b_ref[...], c_ref[...]], axis=1)
    x1_ref[...] = x_ref[...] + jnp.dot(mix, wo_ref[...], preferred_element_type=F32)
    h2 = _rms(x1_ref[...], g2_ref[...]).astype(BF16)
    for f0 in range(0, D_FF, FF_CHUNK):
        f1 = min(f0 + FF_CHUNK, D_FF)
        gate = jnp.dot(h2, wg_ref[:, f0:f1], preferred_element_type=F32)
        up = jnp.dot(h2, wu_ref[:, f0:f1], preferred_element_type=F32)
        act_ref[:, f0:f1] = (gate * (1.0 / (1.0 + jnp.exp(-gate))) * up).astype(BF16)
    x2 = x1_ref[...] + jnp.dot(act_ref[...], wd_ref[...], preferred_element_type=F32)
    o_ref[...] = _rms(x2, gf_ref[...]) if last else x2


def _ffn(x2d, a, b, c, wo, g2, wg, wu, wd, gf, layer, last):
    t = x2d.shape[0]
    rows = lambda w: pl.BlockSpec((TM_FFN, w), lambda i: (i, 0))
    return pl.pallas_call(
        functools.partial(_ffn_kernel, last=last),
        grid=(t // TM_FFN,),
        in_specs=[
            rows(D_MODEL), rows(POOL_WIDTH), rows(DIFF_WIDTH), rows(SB_WIDTH),
            _layer_weight((D_MODEL, D_MODEL), layer), _resident((1, D_MODEL)),
            _layer_weight((D_MODEL, D_FF), layer), _layer_weight((D_MODEL, D_FF), layer),
            _layer_weight((D_FF, D_MODEL), layer),
            _resident((1, D_MODEL)),
        ],
        out_specs=rows(D_MODEL),
        out_shape=jax.ShapeDtypeStruct((t, D_MODEL), F32),
        scratch_shapes=[
            pltpu.VMEM((TM_FFN, D_MODEL), F32),
            pltpu.VMEM((TM_FFN, D_FF), BF16),
        ],
        compiler_params=pltpu.CompilerParams(
            dimension_semantics=("arbitrary",), vmem_limit_bytes=VMEM_LIMIT),
        name="ffn_last" if last else "ffn",
    )(x2d, a, b, c, wo, g2.reshape(1, D_MODEL), wg, wu, wd, gf.reshape(1, D_MODEL))


def kernel(x, norm1_g, w_in, pool_w, pool_scale, lam_q1, lam_k1, lam_q2, lam_k2,
           diff_norm_g, sb_norm_g, w_out, norm2_g, w_gate, w_up, w_down, final_norm_g):
    batch, seq, d = x.shape
    depth = w_in.shape[0]
    assert d == D_MODEL and seq % TQ == 0 and seq % TM_PROJ == 0
    assert (batch * seq) % TM_FFN == 0

    w_in_b, w_out_b = w_in.astype(BF16), w_out.astype(BF16)
    w_gate_b, w_up_b, w_down_b = w_gate.astype(BF16), w_up.astype(BF16), w_down.astype(BF16)
    x2d = x.reshape(batch * seq, d)
    for l in range(depth):
        w_bd = jax.scipy.linalg.block_diag(*[pool_w[l, gi] for gi in range(len(POOL_WINDOWS))])
        a, p3, vt = _proj(x2d, norm1_g[l], w_in_b, l, w_bd.astype(BF16), pool_scale[l], seq)
        lam_init = 0.8 - 0.6 * math.exp(-0.3 * l)
        lamp = jnp.stack([lam_q1[l], lam_k1[l], lam_q2[l], lam_k2[l]])
        bo = _diff_attention(p3, vt, lamp, diff_norm_g[l], lam_init, batch, seq)
        co = _sb_attention(p3, sb_norm_g[l], batch, seq)
        x2d = _ffn(x2d, a, bo, co, w_out_b, norm2_g[l], w_gate_b, w_up_b, w_down_b,
                   final_norm_g, l, last=(l == depth - 1))
    return x2d.reshape(batch, seq, d)
```

```python
import functools
import math

import jax
import jax.numpy as jnp
from jax import lax
from jax.experimental import pallas as pl
from jax.experimental.pallas import tpu as pltpu

F32 = jnp.float32
BF16 = jnp.bfloat16

D_MODEL = 1024
POOL_WINDOWS = (2, 4, 8, 16)
POOL_WIDTH = 256
POOL_GROUP = 64
DIFF_WIDTH = 512
DIFF_HEAD_DIM = 64
DIFF_HEADS = 4
SB_WIDTH = 256
SB_HEAD_DIM = 64
SB_HEADS = 4
D_IN = POOL_WIDTH + 3 * DIFF_WIDTH + 3 * SB_WIDTH
D_FF = 2816
EPS = 1e-6

LANES = 128
N_SLABS = (D_IN - POOL_WIDTH) // LANES
SB_SLABS = SB_WIDTH // LANES
BLK_DQ, BLK_DK = 0, 1
SLAB_DV = 2 * DIFF_HEADS
BLK_SQ, BLK_SK, BLK_SV = 6, 7, 8

TM_PROJ = 512
TM_FFN = 512
POOL_HALO = 16
TQ = 256
TK = 256
FF_CHUNK = 512
VMEM_LIMIT = 56 * 1024 * 1024
DIFF_VDIM = 2 * DIFF_HEAD_DIM
DIFF_Q_SCALE =DIFF_HEAD_DIM ** -0.5 * math.log2(math.e)

_NT = (((1,), (1,)), ((), ()))


def _resident(shape):
    return pl.BlockSpec(shape, lambda *_: (0,) * len(shape), pipeline_mode=pl.Buffered(1))


def _layer_weight(shape, layer):
    return pl.BlockSpec((None,) + shape, lambda *_: (layer, 0, 0), pipeline_mode=pl.Buffered(1))


def _rms(x, g):
    ms = jnp.mean(x * x, axis=-1, keepdims=True)
    return x * lax.rsqrt(ms + EPS) * g


def _pool_mix(e0, first_pos, w_blockdiag, scale):
    s2 = e0 + pltpu.roll(e0, 1, 0)
    s4 = s2 + pltpu.roll(s2, 2, 0)
    s8 = s4 + pltpu.roll(s4, 4, 0)
    s16 = s8 + pltpu.roll(s8, 8, 0)
    shape = e0.shape
    lane = lax.broadcasted_iota(jnp.int32, shape, 1)
    row = lax.broadcasted_iota(jnp.int32, shape, 0)
    g0, g1, g2 = lane < POOL_GROUP, lane < 2 * POOL_GROUP, lane < 3 * POOL_GROUP
    win = jnp.where(g0, s2, jnp.where(g1, s4, jnp.where(g2, s8, s16)))
    width = jnp.where(g0, POOL_WINDOWS[0],
                      jnp.where(g1, POOL_WINDOWS[1],
                                jnp.where(g2, POOL_WINDOWS[2], POOL_WINDOWS[3])))
    pos = first_pos + row - POOL_HALO
    cnt = jnp.maximum(jnp.minimum(pos + 1, width), 1).astype(F32)
    pooled = (win / cnt - e0)[POOL_HALO:]
    mixed = jnp.dot(pooled.astype(BF16), w_blockdiag, preferred_element_type=F32)
    return (mixed * scale).astype(BF16)


def _proj_kernel(x_ref, g_ref, w_ref, wp_ref, sp_ref, a_ref, p_ref, vt_ref, halo_ref,
                 *, steps_per_seq):
    i = pl.program_id(0) % steps_per_seq

    @pl.when(i == 0)
    def _():
        halo_ref[...] = jnp.zeros(halo_ref.shape, F32)

    h = _rms(x_ref[...], g_ref[...]).astype(BF16)
    u = jnp.dot(h, w_ref[:, :POOL_WIDTH], preferred_element_type=F32)
    for c0 in range(0, N_SLABS, 4):
        lo = POOL_WIDTH + c0 * LANES
        n = min(4, N_SLABS - c0)
        acc = jnp.dot(h, w_ref[:, lo:lo + n * LANES], preferred_element_type=F32)
        for j in range(n):
            slab = acc[:, j * LANES:(j + 1) * LANES]
            if c0 + j < DIFF_HEADS:
                slab = slab * DIFF_Q_SCALE
            p_ref[c0 + j] = slab.astype(BF16)
            head = c0 + j - SLAB_DV
            if 0 <= head < DIFF_HEADS:
                for c in range(TM_PROJ // TK):
                    vt_ref[head, c] = slab[c * TK:(c + 1) * TK, :].T.astype(BF16)
    e0 = jnp.concatenate([halo_ref[...], u], axis=0)
    a_ref[...] = _pool_mix(e0, i * TM_PROJ, wp_ref[...], sp_ref[...])
    halo_ref[...] = u[TM_PROJ - POOL_HALO:]


def _proj(x2d, g, w_stack, layer, w_blockdiag, pool_scale, seq):
    t = x2d.shape[0]
    return pl.pallas_call(
        functools.partial(_proj_kernel, steps_per_seq=seq // TM_PROJ),
        grid=(t // TM_PROJ,),
        in_specs=[
            pl.BlockSpec((TM_PROJ, D_MODEL), lambda i: (i, 0)),
            _resident((1, D_MODEL)),
            _layer_weight((D_MODEL, D_IN), layer),
            _resident((POOL_WIDTH, POOL_WIDTH)),
            _resident((1, POOL_WIDTH)),
        ],
        out_specs=[
            pl.BlockSpec((TM_PROJ, POOL_WIDTH), lambda i: (i, 0)),
            pl.BlockSpec((N_SLABS, TM_PROJ, LANES), lambda i: (0, i, 0)),
            pl.BlockSpec((DIFF_HEADS, TM_PROJ // TK, DIFF_VDIM, TK), lambda i: (0, i, 0, 0)),
        ],
        out_shape=[
            jax.ShapeDtypeStruct((t, POOL_WIDTH), BF16),
            jax.ShapeDtypeStruct((N_SLABS, t, LANES), BF16),
            jax.ShapeDtypeStruct((DIFF_HEADS, t // TK, DIFF_VDIM, TK), BF16),
        ],
        scratch_shapes=[pltpu.VMEM((POOL_HALO, POOL_WIDTH), F32)],
        compiler_params=pltpu.CompilerParams(
            dimension_semantics=("arbitrary",), vmem_limit_bytes=VMEM_LIMIT),
        name="proj",
    )(x2d, g.reshape(1, D_MODEL), w_stack, w_blockdiag, pool_scale.reshape(1, POOL_WIDTH))


def _twice(a):
    return jnp.concatenate([a] * (TK // LANES), axis=1)


def _diff_kernel(q_ref, k_ref, vt_ref, lamp_ref, g_ref, o_ref,
                 m_ref, l_ref, acc_ref, s0_ref, s1_ref, *, lam_init):
    qi = pl.program_id(1)
    lane = lax.broadcasted_iota(jnp.int32, (TQ, LANES), 1)
    first_map = lane < DIFF_HEAD_DIM
    qms = []
    for h in range(DIFF_HEADS):
        q = q_ref[h]
        zero = jnp.zeros_like(q)
        qms += [jnp.where(first_map, q, zero), jnp.where(first_map, zero, q)]
    key = lax.broadcasted_iota(jnp.int32, (TK, TQ), 0)
    qry = lax.broadcasted_iota(jnp.int32, (TK, TQ), 1)
    causal = key <= qry

    s_refs = (s0_ref, s1_ref)

    def score_tile(kb, slot):
        start = pl.multiple_of(kb * TK, TK)
        for i, qm in enumerate(qms):
            s_refs[slot][i] = lax.dot_general(k_ref[i // 2, pl.ds(start, TK), :], qm, _NT,
                                              preferred_element_type=F32)

    def softmax_tile(kb, slot, diagonal, next_kb=None):
        if next_kb is not None:
            next_start = pl.multiple_of(next_kb * TK, TK)
        for i, qm in enumerate(qms):
            if next_kb is not None:
                s_refs[1 - slot][i] = lax.dot_general(
                    k_ref[i // 2, pl.ds(next_start, TK), :], qm, _NT, preferred_element_type=F32)
            s = s_refs[slot][i]
            if diagonal:
                s = jnp.where(causal, s, -jnp.inf)
            m_prev = m_ref[i]
            m_new = jnp.maximum(m_prev, jnp.max(s, axis=0, keepdims=True))
            alpha = jnp.exp2(m_prev - m_new)
            p = jnp.exp2(s - m_new)
            l_ref[i] = alpha * l_ref[i] + jnp.sum(p, axis=0, keepdims=True)
            m_ref[i] = m_new
            pv = jnp.dot(vt_ref[i // 2, kb], p.astype(BF16), preferred_element_type=F32)
            acc_ref[i] = alpha * acc_ref[i] + pv

    m_ref[...] = jnp.full(m_ref.shape, -jnp.inf, F32)
    l_ref[...] = jnp.zeros(l_ref.shape, F32)
    acc_ref[...] = jnp.zeros(acc_ref.shape, F32)
    score_tile(qi, 1)
    softmax_tile(qi, 1, True, next_kb=0)

    def run(kb, count):
        for t in range(count):
            softmax_tile(kb + t, t % 2, False, next_kb=jnp.minimum(kb + t + 1, qi))

    def quad(jj, carry):
        run(4 * jj, 4)
        return carry

    lax.fori_loop(0, qi // 4, quad, 0)
    rest = (qi // 4) * 4
    pl.when(qi % 4 >= 2)(functools.partial(run, rest, 2))
    pl.when(qi % 2 == 1)(lambda: softmax_tile(qi - 1, 0, False))

    lamp = lamp_ref[...]
    lam = (jnp.exp(jnp.sum(lamp[0:1] * lamp[1:2], axis=1, keepdims=True))
           - jnp.exp(jnp.sum(lamp[2:3] * lamp[3:4], axis=1, keepdims=True)) + lam_init)
    for h in range(DIFF_HEADS):
        o = acc_ref[2 * h] / l_ref[2 * h] - lam * (acc_ref[2 * h + 1] / l_ref[2 * h + 1])
        ms = jnp.mean(o * o, axis=0, keepdims=True)
        y = o * lax.rsqrt(ms + EPS) * g_ref[...] * (1.0 - lam_init)
        o_ref[:, h * LANES:(h + 1) * LANES] = y.T.astype(BF16)


def _diff_attention(p3, vt, lamp, g, lam_init, batch, seq):
    t = p3.shape[1]
    nq = seq // TQ
    nk = seq // TK
    return pl.pallas_call(
        functools.partial(_diff_kernel, lam_init=lam_init),
        grid=(batch, nq),
        in_specs=[
            pl.BlockSpec((DIFF_HEADS, TQ, LANES), lambda b, i: (BLK_DQ, b * nq + i, 0)),
            pl.BlockSpec((DIFF_HEADS, seq, LANES), lambda b, i: (BLK_DK, b, 0)),
            pl.BlockSpec((DIFF_HEADS, nk, DIFF_VDIM, TK), lambda b, i: (0, b, 0, 0)),
            _resident((4, DIFF_HEAD_DIM)),
            _resident((DIFF_VDIM, 1)),
        ],
        out_specs=pl.BlockSpec((TQ, DIFF_WIDTH), lambda b, i: (b * nq + i, 0)),
        out_shape=jax.ShapeDtypeStruct((t, DIFF_WIDTH), BF16),
        scratch_shapes=[
            pltpu.VMEM((2 * DIFF_HEADS, 1, TQ), F32),
            pltpu.VMEM((2 * DIFF_HEADS, 1, TQ), F32),
            pltpu.VMEM((2 * DIFF_HEADS, DIFF_VDIM, TQ), F32),
            pltpu.VMEM((2 * DIFF_HEADS, TK, TQ), F32),
            pltpu.VMEM((2 * DIFF_HEADS, TK, TQ), F32),
        ],
        compiler_params=pltpu.CompilerParams(
            dimension_semantics=("arbitrary",) * 2, vmem_limit_bytes=VMEM_LIMIT),
        name="diff_attn",
    )(p3, p3, vt, lamp, g.reshape(DIFF_VDIM, 1))


def _sb_kernel(q_ref, k_ref, v_ref, g_ref, o_ref, tri_ref, carry_ref, acc_ref, z0_ref, z1_ref):
    b, qi = pl.program_id(0), pl.program_id(1)

    @pl.when((b == 0) & (qi == 0))
    def _():
        j = lax.broadcasted_iota(jnp.int32, (2 * TK, TK), 0)
        s = lax.broadcasted_iota(jnp.int32, (2 * TK, TK), 1)
        j = jnp.where(j >= TK, j - TK, j)
        tri_ref[...] = jnp.where(j >= s, 1.0, 0.0).astype(BF16)

    scale = jnp.asarray(SB_HEAD_DIM ** -0.5, BF16)
    lane = lax.broadcasted_iota(jnp.int32, (TQ, LANES), 1)
    lower = lane < SB_HEAD_DIM
    qms = []
    for sl in range(SB_SLABS):
        q = q_ref[sl] * scale
        zero = jnp.zeros_like(q)
        qms += [jnp.where(lower, q, zero), jnp.where(lower, zero, q)]
    row = lax.broadcasted_iota(jnp.int32, (TQ, TK), 0)
    col = lax.broadcasted_iota(jnp.int32, (TQ, TK), 1)
    strict = col < row

    z_refs = (z0_ref, z1_ref)

    def score_tile(kb, slot):
        start = pl.multiple_of(kb * TK, TK)
        for hd, qm in enumerate(qms):
            z_refs[slot][hd] = lax.dot_general(qm, k_ref[hd // 2, pl.ds(start, TK), :], _NT,
                                               preferred_element_type=F32)

    def weight_tile(kb, slot, diagonal):
        start = pl.multiple_of(kb * TK, TK)
        zs = [z_refs[slot][hd] for hd in range(SB_HEADS)]
        splits = []
        for z in zs:
            sp = jnp.maximum(z, 0.0) + jnp.log(1.0 + jnp.exp(-jnp.abs(z)))
            if diagonal:
                sp = jnp.where(strict, sp, 0.0)
            hi = sp.astype(BF16)
            lo = (sp - hi.astype(F32)).astype(BF16)
            splits.append(jnp.concatenate([hi, lo], axis=1))
        sums = [jnp.dot(s, tri_ref[...], preferred_element_type=F32) for s in splits]
        weights = []
        for hd, (z, c) in enumerate(zip(zs, sums)):
            if not diagonal:
                c = c + _twice(carry_ref[hd])
            a = jnp.exp(z - c)
            if diagonal:
                a = jnp.where(strict, a, 0.0)
            carry_ref[hd] = jnp.broadcast_to(c[:, 0:1], (TQ, LANES))
            weights.append(a.astype(BF16))
        for hd, a in enumerate(weights):
            pv = jnp.dot(a, v_ref[hd // 2, pl.ds(start, TK), :], preferred_element_type=F32)
            acc_ref[hd] = pv if diagonal else acc_ref[hd] + pv

    score_tile(qi, 0)
    score_tile(jnp.maximum(qi - 1, 0), 1)
    weight_tile(qi, 0, True)

    def step(j, slot):
        kb = qi - 1 - j
        score_tile(jnp.maximum(kb - 1, 0), 1 - slot)
        weight_tile(kb, slot, False)

    def body(j, carry):
        for slot in range(2):
            pl.when((j + 1) % 2 == slot)(functools.partial(step, j, slot))
        return carry

    lax.fori_loop(0, qi, body, 0)

    lane_o = lax.broadcasted_iota(jnp.int32, (TQ, LANES), 1)
    first = lane_o < SB_HEAD_DIM
    for sl in range(SB_SLABS):
        o = jnp.where(first, acc_ref[2 * sl], acc_ref[2 * sl + 1])
        sq = o * o
        ms = jnp.where(first,
                       jnp.sum(jnp.where(first, sq, 0.0), axis=1, keepdims=True),
                       jnp.sum(jnp.where(first, 0.0, sq), axis=1, keepdims=True)) / SB_HEAD_DIM
        o_ref[:, sl * LANES:(sl + 1) * LANES] = (
            o * lax.rsqrt(ms + EPS) * g_ref[...]).astype(BF16)


def _sb_attention(p3, g, batch, seq):
    t = p3.shape[1]
    nq = seq // TQ
    return pl.pallas_call(
        _sb_kernel,
        grid=(batch, nq),
        in_specs=[
            pl.BlockSpec((SB_SLABS, TQ, LANES), lambda b, i: (BLK_SQ, b * nq + i, 0)),
            pl.BlockSpec((SB_SLABS, seq, LANES), lambda b, i: (BLK_SK, b, 0)),
            pl.BlockSpec((SB_SLABS, seq, LANES), lambda b, i: (BLK_SV, b, 0)),
            _resident((1, LANES)),
        ],
        out_specs=pl.BlockSpec((TQ, SB_WIDTH), lambda b, i: (b * nq + i, 0)),
        out_shape=jax.ShapeDtypeStruct((t, SB_WIDTH), BF16),
        scratch_shapes=[
            pltpu.VMEM((2 * TK, TK), BF16),
            pltpu.VMEM((SB_HEADS, TQ, LANES), F32),
            pltpu.VMEM((SB_HEADS, TQ, LANES), F32),
            pltpu.VMEM((SB_HEADS, TQ, TK), F32),
            pltpu.VMEM((SB_HEADS, TQ, TK), F32),
        ],
        compiler_params=pltpu.CompilerParams(
            dimension_semantics=("arbitrary",) * 2, vmem_limit_bytes=VMEM_LIMIT),
        name="sb_attn",
    )(p3, p3, p3, jnp.concatenate([g, g]).reshape(1, LANES))


def _ffn_kernel(x_ref, a_ref, b_ref, c_ref, wo_ref, g2_ref, wg_ref, wu_ref, wd_ref, gf_ref,
                o_ref, x1_ref, act_ref, *, last):
    mix = jnp.concatenate([a_ref[...], b_ref[...], c_ref[...]], axis=1)
    x1_ref[...] = x_ref[...] + jnp.dot(mix, wo_ref[...], preferred_element_type=F32)
    h2 = _rms(x1_ref[...], g2_ref[...]).astype(BF16)
    for f0 in range(0, D_FF, FF_CHUNK):
        f1 = min(f0 + FF_CHUNK, D_FF)
        gate = jnp.dot(h2, wg_ref[:, f0:f1], preferred_element_type=F32)
        up = jnp.dot(h2, wu_ref[:, f0:f1], preferred_element_type=F32)
        act_ref[:, f0:f1] = (gate * (1.0 / (1.0 + jnp.exp(-gate))) * up).astype(BF16)
    x2 = x1_ref[...] + jnp.dot(act_ref[...], wd_ref[...], preferred_element_type=F32)
    o_ref[...] = _rms(x2, gf_ref[...]) if last else x2


def _ffn(x2d, a, b, c, wo, g2, wg, wu, wd, gf, layer, last):
    t = x2d.shape[0]
    rows = lambda w: pl.BlockSpec((TM_FFN, w), lambda i: (i, 0))
    return pl.pallas_call(
        functools.partial(_ffn_kernel, last=last),
        grid=(t // TM_FFN,),
        in_specs=[
            rows(D_MODEL), rows(POOL_WIDTH), rows(DIFF_WIDTH), rows(SB_WIDTH),
            _layer_weight((D_MODEL, D_MODEL), layer), _resident((1, D_MODEL)),
            _layer_weight((D_MODEL, D_FF), layer), _layer_weight((D_MODEL, D_FF), layer),
            _layer_weight((D_FF, D_MODEL), layer),
            _resident((1, D_MODEL)),
        ],
        out_specs=rows(D_MODEL),
        out_shape=jax.ShapeDtypeStruct((t, D_MODEL), F32),
        scratch_shapes=[
            pltpu.VMEM((TM_FFN, D_MODEL), F32),
            pltpu.VMEM((TM_FFN, D_FF), BF16),
        ],
        compiler_params=pltpu.CompilerParams(
            dimension_semantics=("arbitrary",), vmem_limit_bytes=VMEM_LIMIT),
        name="ffn_last" if last else "ffn",
    )(x2d, a, b, c, wo, g2.reshape(1, D_MODEL), wg, wu, wd, gf.reshape(1, D_MODEL))


def kernel(x, norm1_g, w_in, pool_w, pool_scale, lam_q1, lam_k1, lam_q2, lam_k2,
           diff_norm_g, sb_norm_g, w_out, norm2_g, w_gate, w_up, w_down, final_norm_g):
    batch, seq, d = x.shape
    depth = w_in.shape[0]
    assert d == D_MODEL and seq % TQ == 0 and seq % TM_PROJ == 0
    assert (batch * seq) % TM_FFN == 0

    w_in_b, w_out_b = w_in.astype(BF16), w_out.astype(BF16)
    w_gate_b, w_up_b, w_down_b = w_gate.astype(BF16), w_up.astype(BF16), w_down.astype(BF16)
    x2d = x.reshape(batch * seq, d)
    for l in range(depth):
        w_bd = jax.scipy.linalg.block_diag(*[pool_w[l, gi] for gi in range(len(POOL_WINDOWS))])
        a, p3, vt = _proj(x2d, norm1_g[l], w_in_b, l, w_bd.astype(BF16), pool_scale[l], seq)
        lam_init = 0.8 - 0.6 * math.exp(-0.3 * l)
        lamp = jnp.stack([lam_q1[l], lam_k1[l], lam_q2[l], lam_k2[l]])
        bo = _diff_attention(p3, vt, lamp, diff_norm_g[l], lam_init, batch, seq)
        co = _sb_attention(p3, sb_norm_g[l], batch, seq)
        x2d = _ffn(x2d, a, bo, co, w_out_b, norm2_g[l], w_gate_b, w_up_b, w_down_b,
                   final_norm_g, l, last=(l == depth - 1))
    return x2d.reshape(batch, seq, d)
```
